```python
import jax, jax.numpy as jnp
from jax import lax
import numpy as np

D_MODEL = 1024
BATCH = 8
SEQ = 4096
DEPTH = 4

D_FF = 2816
FFN_RESID = 0.5
EPS = 1e-6
GM_WIDTH = 512
GM_GROUPS = 4
GM_GROUP_DIM = GM_WIDTH // GM_GROUPS
GM_CHUNK = 128
MLA_HEADS = 8
MLA_Q_RANK = 384
MLA_KV_RANK = 256
MLA_NOPE = 64
MLA_ROPE = 32
MLA_QK_DIM = MLA_NOPE + MLA_ROPE
MLA_V = 64
MLA_WIDTH = MLA_HEADS * MLA_V
ROPE_THETA = 10000.0
Q_BLOCK = 128
SSD_HEADS = 8
SSD_HEAD_DIM = 64
SSD_INNER = SSD_HEADS * SSD_HEAD_DIM
SSD_GROUPS = 2
SSD_STATE = 128
SSD_CONV = 4
SSD_CHUNK = 128
SSD_CONV_DIM = SSD_INNER + 2 * SSD_GROUPS * SSD_STATE
N_BRANCH = 3
BRANCH_WIDTH = 512
IN_WIDTHS = (2 * GM_WIDTH, MLA_Q_RANK, MLA_KV_RANK, MLA_ROPE, SSD_INNER, SSD_CONV_DIM, SSD_HEADS, N_BRANCH * D_MODEL)
IN_COLS = sum(IN_WIDTHS)
IN_OFFSETS = tuple(int(v) for v in np.cumsum(IN_WIDTHS)[:-1])

kernel_name = 'hybrid_gmlp_mla_ssd_macaron'


def rms_norm(x, gain):
    xf = x.astype(jnp.float32)
    y = xf * lax.rsqrt(jnp.mean(xf * xf, axis=-1, keepdims=True) + EPS)
    return (y * gain.astype(jnp.float32)).astype(x.dtype)


def swiglu_ffn(h, w_in, w_out):
    gate, up = jnp.split(h @ w_in, 2, axis=-1)
    return (jax.nn.silu(gate) * up) @ w_out


def apply_rope(x, cos, sin):
    x1, x2 = jnp.split(x, 2, axis=-1)
    return jnp.concatenate([x1 * cos - x2 * sin, x2 * cos + x1 * sin], axis=-1)


def gmlp_mixer(uv, v_gain, w_s, b_s):
    bsz, s, _ = uv.shape
    u, v = jnp.split(jax.nn.gelu(uv, approximate=False), 2, axis=-1)
    v = rms_norm(v, v_gain).reshape(bsz, s // GM_CHUNK, GM_CHUNK, GM_GROUPS, GM_GROUP_DIM)
    causal = jnp.tril(jnp.ones((GM_CHUNK, GM_CHUNK), dtype=bool))
    w = jnp.where(causal[None], w_s, 0.0).astype(v.dtype)
    sp = jnp.einsum('gts,bcsgd->bctgd', w, v) + b_s.T[:, :, None].astype(v.dtype)
    return u * sp.reshape(bsz, s, GM_WIDTH)


def blocked_causal_attention(q, k, v):
    bsz, s, nh, dqk = q.shape
    dv = v.shape[-1]
    nb = s // Q_BLOCK
    scale = dqk ** -0.5
    qb = q.reshape(bsz, nb, Q_BLOCK, nh, dqk).transpose(1, 0, 3, 2, 4)
    kt = k.transpose(0, 2, 1, 3)
    vt = v.transpose(0, 2, 1, 3)
    kpos = jnp.arange(s)

    def one_block(args):
        qi, i = args
        sc = jnp.einsum('bhqd,bhkd->bhqk', qi, kt, preferred_element_type=jnp.float32) * scale
        qpos = i * Q_BLOCK + jnp.arange(Q_BLOCK)
        sc = jnp.where(kpos[None, :] <= qpos[:, None], sc, -jnp.inf)
        p = jax.nn.softmax(sc, axis=-1).astype(vt.dtype)
        return jnp.einsum('bhqk,bhkd->bhqd', p, vt)

    out = lax.map(one_block, (qb, jnp.arange(nb)))
    return out.transpose(1, 0, 3, 2, 4).reshape(bsz, s, nh * dv)


def mla_mixer(c_q, c_kv, k_rope, cos, sin, q_norm, kv_norm, w_uq, w_ukv, q_gain, k_gain):
    bsz, s, _ = c_q.shape
    q = (rms_norm(c_q, q_norm) @ w_uq).reshape(bsz, s, MLA_HEADS, MLA_QK_DIM)
    kv = (rms_norm(c_kv, kv_norm) @ w_ukv).reshape(bsz, s, MLA_HEADS, MLA_NOPE + MLA_V)
    k_nope, v = jnp.split(kv, [MLA_NOPE], axis=-1)
    k_pe = jnp.broadcast_to(k_rope[:, :, None, :], (bsz, s, MLA_HEADS, MLA_ROPE))
    k = jnp.concatenate([k_nope, k_pe], axis=-1)
    q = rms_norm(q, q_gain)
    k = rms_norm(k, k_gain)
    q = jnp.concatenate([q[..., :MLA_NOPE], apply_rope(q[..., MLA_NOPE:], cos, sin)], axis=-1)
    k = jnp.concatenate([k[..., :MLA_NOPE], apply_rope(k[..., MLA_NOPE:], cos, sin)], axis=-1)
    return blocked_causal_attention(q, k, v)


def segsum(a):
    t = a.shape[-1]
    idx = jnp.arange(t)
    ax = jnp.where(idx[:, None] > idx[None, :], a[..., :, None], 0.0)
    ss = jnp.cumsum(ax, axis=-2)
    return jnp.where(idx[:, None] >= idx[None, :], ss, -jnp.inf)


def ssd_scan(xs, dt, a_log, b_in, c_in):
    bsz, s, nh, hp = xs.shape
    ng, ns = b_in.shape[2], b_in.shape[3]
    nr = nh // ng
    nc = s // SSD_CHUNK
    a = -jnp.exp(a_log.astype(jnp.float32))
    da = (dt * a).reshape(bsz, nc, SSD_CHUNK, ng, nr).transpose(0, 3, 4, 1, 2)
    xdt = (xs * dt[..., None].astype(xs.dtype)).reshape(bsz, nc, SSD_CHUNK, ng, nr, hp)
    bc = b_in.reshape(bsz, nc, SSD_CHUNK, ng, ns)
    cc = c_in.reshape(bsz, nc, SSD_CHUNK, ng, ns)
    cs = jnp.cumsum(da, axis=-1)
    dt_ = xs.dtype
    lmat = jnp.exp(segsum(da)).astype(dt_)
    cb = jnp.einsum('bclgn,bcsgn->bgcls', cc, bc)
    y_diag = jnp.einsum('bgrcls,bcsgrp->bclgrp', cb[:, :, None] * lmat, xdt)
    decay_states = jnp.exp(cs[..., -1:] - cs).astype(dt_)
    states = jnp.einsum('bclgn,bgrcl,bclgrp->bcgrpn', bc, decay_states, xdt)
    chunk_tot = jnp.pad(cs[..., -1], ((0, 0), (0, 0), (0, 0), (1, 0)))
    decay_chunk = jnp.exp(segsum(chunk_tot)).astype(dt_)
    states0 = jnp.concatenate([jnp.zeros_like(states[:, :1]), states], axis=1)
    new_states = jnp.einsum('bgrzc,bcgrpn->bzgrpn', decay_chunk, states0)
    states_in = new_states[:, :-1]
    y_off = jnp.einsum('bclgn,bcgrpn,bgrcl->bclgrp', cc, states_in, jnp.exp(cs).astype(dt_))
    return (y_diag + y_off).reshape(bsz, s, nh, hp)


def ssd_mixer(z, xbc, dt_raw, conv_w, conv_b, dt_bias, a_log, d_skip, norm_gain):
    bsz, s, _ = xbc.shape
    xbc = lax.conv_general_dilated(xbc, conv_w[:, None, :].astype(xbc.dtype), (1,), [(SSD_CONV - 1, 0)],
                                   dimension_numbers=('NWC', 'WIO', 'NWC'), feature_group_count=SSD_CONV_DIM)
    xbc = jax.nn.silu(xbc + conv_b.astype(xbc.dtype))
    xs, b_in, c_in = jnp.split(xbc, [SSD_INNER, SSD_INNER + SSD_GROUPS * SSD_STATE], axis=-1)
    xs = xs.reshape(bsz, s, SSD_HEADS, SSD_HEAD_DIM)
    b_in = b_in.reshape(bsz, s, SSD_GROUPS, SSD_STATE)
    c_in = c_in.reshape(bsz, s, SSD_GROUPS, SSD_STATE)
    dt = jax.nn.softplus(dt_raw.astype(jnp.float32) + dt_bias.astype(jnp.float32))
    y = ssd_scan(xs, dt, a_log, b_in, c_in) + xs * d_skip[:, None].astype(xs.dtype)
    y = y.reshape(bsz, s, SSD_INNER) * jax.nn.silu(z)
    y = rms_norm(y.reshape(bsz, s, SSD_GROUPS, SSD_INNER // SSD_GROUPS), norm_gain.reshape(SSD_GROUPS, -1))
    return y.reshape(bsz, s, SSD_INNER)


def setup_inputs(seed: int = 0) -> dict:
    key = jax.random.key(seed)
    ks = jax.random.split(key, 32)
    f32 = jnp.float32

    def nrm(k, shape, scale):
        return jax.random.normal(k, shape, f32) * scale

    def gain(k, shape):
        return 1.0 + 0.05 * jax.random.normal(k, shape, f32)

    L = DEPTH
    x = jax.random.normal(ks[0], (BATCH, SEQ, D_MODEL), f32)
    offsets = jax.random.randint(ks[1], (BATCH, 1), 0, SEQ, dtype=jnp.int32)
    positions = offsets + jnp.arange(SEQ, dtype=jnp.int32)[None, :]
    dt0 = jnp.exp(jax.random.uniform(ks[2], (L, SSD_HEADS), f32, np.log(1e-3), np.log(1e-1)))
    return {
        'x': x,
        'positions': positions,
        'ffn1_norm': gain(ks[3], (L, D_MODEL)),
        'ffn1_w_in': nrm(ks[4], (L, D_MODEL, 2 * D_FF), D_MODEL ** -0.5),
        'ffn1_w_out': nrm(ks[5], (L, D_FF, D_MODEL), D_FF ** -0.5),
        'mix_norm': gain(ks[6], (L, D_MODEL)),
        'w_in': nrm(ks[7], (L, D_MODEL, IN_COLS), D_MODEL ** -0.5),
        'gm_v_norm': gain(ks[8], (L, GM_WIDTH)),
        'gm_w_s': nrm(ks[9], (L, GM_GROUPS, GM_CHUNK, GM_CHUNK), 0.5 * GM_CHUNK ** -0.5),
        'gm_b_s': 1.0 + 0.1 * jax.random.normal(ks[10], (L, GM_GROUPS, GM_CHUNK), f32),
        'mla_q_norm': gain(ks[11], (L, MLA_Q_RANK)),
        'mla_kv_norm': gain(ks[12], (L, MLA_KV_RANK)),
        'mla_w_uq': nrm(ks[13], (L, MLA_Q_RANK, MLA_HEADS * MLA_QK_DIM), MLA_Q_RANK ** -0.5),
        'mla_w_ukv': nrm(ks[14], (L, MLA_KV_RANK, MLA_HEADS * (MLA_NOPE + MLA_V)), MLA_KV_RANK ** -0.5),
        'mla_q_gain': gain(ks[15], (L, MLA_QK_DIM)),
        'mla_k_gain': gain(ks[16], (L, MLA_QK_DIM)),
        'ssd_conv_w': nrm(ks[17], (L, SSD_CONV, SSD_CONV_DIM), SSD_CONV ** -0.5),
        'ssd_conv_b': nrm(ks[18], (L, SSD_CONV_DIM), 0.02),
        'ssd_dt_bias': dt0 + jnp.log(-jnp.expm1(-dt0)),
        'ssd_a_log': jnp.log(jax.random.uniform(ks[19], (L, SSD_HEADS), f32, 1.0, 16.0)),
        'ssd_d': 1.0 + 0.1 * jax.random.normal(ks[20], (L, SSD_HEADS), f32),
        'ssd_norm': gain(ks[21], (L, SSD_INNER)),
        'w_branch': nrm(ks[22], (L, N_BRANCH, BRANCH_WIDTH, D_MODEL), BRANCH_WIDTH ** -0.5),
        'w_out': nrm(ks[23], (L, D_MODEL, D_MODEL), D_MODEL ** -0.5),
        'ffn2_norm': gain(ks[24], (L, D_MODEL)),
        'ffn2_w_in': nrm(ks[25], (L, D_MODEL, 2 * D_FF), D_MODEL ** -0.5),
        'ffn2_w_out': nrm(ks[26], (L, D_FF, D_MODEL), D_FF ** -0.5),
    }


def reference(x, positions, ffn1_norm, ffn1_w_in, ffn1_w_out, mix_norm, w_in, gm_v_norm, gm_w_s, gm_b_s,
              mla_q_norm, mla_kv_norm, mla_w_uq, mla_w_ukv, mla_q_gain, mla_k_gain,
              ssd_conv_w, ssd_conv_b, ssd_dt_bias, ssd_a_log, ssd_d, ssd_norm,
              w_branch, w_out, ffn2_norm, ffn2_w_in, ffn2_w_out):
    bsz, s, _ = x.shape
    inv_freq = 1.0 / (ROPE_THETA ** (jnp.arange(0, MLA_ROPE, 2, dtype=jnp.float32) / MLA_ROPE))
    ang = positions.astype(jnp.float32)[..., None] * inv_freq
    cos = jnp.cos(ang)[:, :, None, :].astype(x.dtype)
    sin = jnp.sin(ang)[:, :, None, :].astype(x.dtype)
    for l in range(DEPTH):
        x = x + FFN_RESID * swiglu_ffn(rms_norm(x, ffn1_norm[l]), ffn1_w_in[l], ffn1_w_out[l])
        h = rms_norm(x, mix_norm[l])
        uv, c_q, c_kv, k_rope, z, xbc, dt_raw, gates = jnp.split(h @ w_in[l], IN_OFFSETS, axis=-1)
        y_a = gmlp_mixer(uv, gm_v_norm[l], gm_w_s[l], gm_b_s[l])
        y_b = mla_mixer(c_q, c_kv, k_rope, cos, sin, mla_q_norm[l], mla_kv_norm[l], mla_w_uq[l], mla_w_ukv[l],
                        mla_q_gain[l], mla_k_gain[l])
        y_c = ssd_mixer(z, xbc, dt_raw, ssd_conv_w[l], ssd_conv_b[l], ssd_dt_bias[l], ssd_a_log[l], ssd_d[l],
                        ssd_norm[l])
        g = jax.nn.sigmoid(gates).reshape(bsz, s, N_BRANCH, D_MODEL)
        merged = (g[:, :, 0] * (y_a @ w_branch[l, 0])
                  + g[:, :, 1] * (y_b @ w_branch[l, 1])
                  + g[:, :, 2] * (y_c @ w_branch[l, 2]))
        x = x + merged @ w_out[l]
        x = x + FFN_RESID * swiglu_ffn(rms_norm(x, ffn2_norm[l]), ffn2_w_in[l], ffn2_w_out[l])
    return x
```

```python
import functools

import jax
import jax.numpy as jnp
from jax import lax
from jax.experimental import pallas as pl
from jax.experimental.pallas import tpu as pltpu

F32 = jnp.float32
BF16 = jnp.bfloat16

LANES = 128
VMEM_LIMIT_BYTES = 56 * 1024 * 1024

EPS = 1e-6
FFN_RESID = 0.5
GM_GROUPS = 4
GM_CHUNK = 128
MLA_HEADS = 8
MLA_NOPE = 64
MLA_ROPE = 32
MLA_QK_DIM = MLA_NOPE + MLA_ROPE
MLA_V = 64
ROPE_THETA = 10000.0
SSD_HEADS = 8
SSD_HEAD_DIM = 64
SSD_GROUPS = 2
SSD_STATE = 128
SSD_CONV = 4
SSD_CHUNK = 128
N_BRANCH = 3
DT_LANE0 = 0
ROPE_LANE0 = MLA_NOPE

ROW_TILE = 512
ATTN_TQ = 512
ATTN_TK = 512
SSD_TILE = 512
FF_CHUNK = 256


def _rms(x, gain):
    ms = jnp.mean(x * x, axis=-1, keepdims=True)
    return x * lax.rsqrt(ms + EPS) * gain


def _gelu(x):
    return 0.5 * x * (1.0 + lax.erf(x * (2.0 ** -0.5)))


def _resident(shape, layer):
    zeros = (0,) * (len(shape) - 1)
    return pl.BlockSpec((None,) + tuple(shape[1:]), lambda *_: (layer,) + zeros, pipeline_mode=pl.Buffered(1))


def _rows(width, tile=ROW_TILE):
    return pl.BlockSpec((tile, width), lambda i: (i, 0))


def _params(*sem):
    return pltpu.CompilerParams(dimension_semantics=sem, vmem_limit_bytes=VMEM_LIMIT_BYTES)


def _rope_table_kernel(pos_ref, freq_ref, sign_ref, cos_ref, sin_ref):
    ang = pos_ref[...].astype(F32) * freq_ref[...]
    cos_ref[...] = jnp.cos(ang)
    sin_ref[...] = jnp.sin(ang) * sign_ref[...]


def _rope_tables(positions):
    n = positions.size
    inv_freq = 1.0 / (ROPE_THETA ** (jnp.arange(0, MLA_ROPE, 2, dtype=F32) / MLA_ROPE))
    half = MLA_ROPE // 2
    freq = jnp.zeros((1, LANES), F32)
    freq = freq.at[0, ROPE_LANE0:ROPE_LANE0 + half].set(inv_freq).at[0, ROPE_LANE0 + half:ROPE_LANE0 + 2 * half].set(inv_freq)
    sign = jnp.zeros((1, LANES), F32)
    sign = sign.at[0, ROPE_LANE0:ROPE_LANE0 + half].set(-1.0).at[0, ROPE_LANE0 + half:ROPE_LANE0 + 2 * half].set(1.0)
    tile = min(2048, n)
    return pl.pallas_call(
        _rope_table_kernel,
        grid=(n // tile,),
        in_specs=[pl.BlockSpec((tile, 1), lambda i: (i, 0)),
                  pl.BlockSpec((1, LANES), lambda i: (0, 0)),
                  pl.BlockSpec((1, LANES), lambda i: (0, 0))],
        out_specs=[pl.BlockSpec((tile, LANES), lambda i: (i, 0))] * 2,
        out_shape=[jax.ShapeDtypeStruct((n, LANES), F32)] * 2,
        compiler_params=_params("parallel"),
        name="rope_tables",
    )(positions.reshape(n, 1), freq, sign)


def _ffn_body(x, gain, w_in_ref, w_out_ref, hid_ref):
    d_ff = w_out_ref.shape[0]
    h = _rms(x, gain).astype(BF16)
    for c in range(d_ff // FF_CHUNK):
        lo = c * FF_CHUNK
        g = jnp.dot(h, w_in_ref[:, lo:lo + FF_CHUNK], preferred_element_type=F32)
        u = jnp.dot(h, w_in_ref[:, d_ff + lo:d_ff + lo + FF_CHUNK], preferred_element_type=F32)
        hid_ref[:, lo:lo + FF_CHUNK] = (g * jax.nn.sigmoid(g) * u).astype(BF16)
    y = jnp.dot(hid_ref[...], w_out_ref[...], preferred_element_type=F32)
    return x + FFN_RESID * y


def _ffn_kernel(x_ref, gain_ref, w_in_ref, w_out_ref, o_ref, hid_ref):
    o_ref[...] = _ffn_body(x_ref[...], gain_ref[...], w_in_ref, w_out_ref, hid_ref)


def _ffn(x, gain, w_in, w_out, layer):
    n, d = x.shape
    d_ff = w_out.shape[1]
    assert d_ff % FF_CHUNK == 0 and n % ROW_TILE == 0
    return pl.pallas_call(
        _ffn_kernel,
        grid=(n // ROW_TILE,),
        in_specs=[_rows(d), _resident(gain.shape, layer), _resident(w_in.shape, layer), _resident(w_out.shape, layer)],
        out_specs=_rows(d),
        out_shape=jax.ShapeDtypeStruct((n, d), F32),
        scratch_shapes=[pltpu.VMEM((ROW_TILE, d_ff), BF16)],
        compiler_params=_params("parallel"),
        name="ffn",
    )(x, gain, w_in, w_out)


def _lane_iota(shape):
    return lax.broadcasted_iota(jnp.int32, shape, len(shape) - 1)


def _rope(x, cosf, sinf, lane):
    half = MLA_ROPE // 2
    swapped = jnp.where(lane < ROPE_LANE0 + half, pltpu.roll(x, LANES - half, 1), pltpu.roll(x, half, 1))
    return x * cosf + swapped * sinf


def _proj_kernel(x_ref, cos_ref, sin_ref, mix_gain_ref, w_proj_ref,
                 gm_gain_ref, gm_w_ref, gm_b_ref,
                 q_norm_ref, kv_norm_ref, w_uq_ref, w_uk_ref, w_uv_ref, q_gain_ref, k_gain_ref,
                 ya_ref, q_ref, k_ref, v_ref, small_ref, z_ref, xbc_ref,
                 *, widths):
    gm_w, q_rank, kv_rank, ssd_inner, conv_dim = widths
    tile = x_ref.shape[0]
    h = _rms(x_ref[...], mix_gain_ref[...]).astype(BF16)

    def proj(lo, width):
        return jnp.dot(h, w_proj_ref[:, lo:lo + width], preferred_element_type=F32)

    off_uv = 0
    off_cq = off_uv + 2 * gm_w
    off_ckv = off_cq + q_rank
    off_small = off_ckv + kv_rank
    off_z = off_small + LANES
    off_xbc = off_z + ssd_inner

    small = proj(off_small, LANES)
    small_ref[...] = small
    z_ref[...] = proj(off_z, ssd_inner).astype(BF16)
    xbc_ref[...] = proj(off_xbc, conv_dim).astype(BF16)

    u = _gelu(proj(off_uv, gm_w))
    v = _gelu(proj(off_uv + gm_w, gm_w))
    v = _rms(v, gm_gain_ref[...]).astype(BF16)
    n_chunk = tile // GM_CHUNK
    gdim = gm_w // GM_GROUPS
    row = lax.broadcasted_iota(jnp.int32, (GM_CHUNK, GM_CHUNK), 0)
    col = lax.broadcasted_iota(jnp.int32, (GM_CHUNK, GM_CHUNK), 1)
    for g in range(GM_GROUPS):
        w = jnp.where(col <= row, gm_w_ref[g], 0.0).astype(BF16)
        rhs = jnp.concatenate([v[c * GM_CHUNK:(c + 1) * GM_CHUNK, g * gdim:(g + 1) * gdim] for c in range(n_chunk)], axis=1)
        sp = jnp.dot(w, rhs, preferred_element_type=F32)
        bias = gm_b_ref[:, g * gdim:(g + 1) * gdim]
        for c in range(n_chunk):
            rows = slice(c * GM_CHUNK, (c + 1) * GM_CHUNK)
            ya_ref[rows, g * gdim:(g + 1) * gdim] = (
                u[rows, g * gdim:(g + 1) * gdim] * (sp[:, c * gdim:(c + 1) * gdim] + bias)).astype(BF16)

    cosf = cos_ref[...]
    sinf = sin_ref[...]
    lane = _lane_iota((tile, LANES))
    qn = _rms(proj(off_cq, q_rank), q_norm_ref[...]).astype(BF16)
    kvn = _rms(proj(off_ckv, kv_rank), kv_norm_ref[...]).astype(BF16)
    v_ref[...] = jnp.dot(kvn, w_uv_ref[...], preferred_element_type=F32).astype(BF16)
    k_rope = jnp.where((lane >= ROPE_LANE0) & (lane < ROPE_LANE0 + MLA_ROPE), small, 0.0)
    q_gain = q_gain_ref[...]
    k_gain = k_gain_ref[...]
    inv_dim = 1.0 / MLA_QK_DIM
    for hd in range(MLA_HEADS):
        lanes = slice(hd * LANES, (hd + 1) * LANES)
        qh = jnp.dot(qn, w_uq_ref[:, lanes], preferred_element_type=F32)
        qh = qh * lax.rsqrt(jnp.sum(qh * qh, axis=-1, keepdims=True) * inv_dim + EPS) * q_gain
        q_ref[:, lanes] = _rope(qh, cosf, sinf, lane).astype(BF16)
        kh = jnp.dot(kvn, w_uk_ref[:, lanes], preferred_element_type=F32) + k_rope
        kh = kh * lax.rsqrt(jnp.sum(kh * kh, axis=-1, keepdims=True) * inv_dim + EPS) * k_gain
        k_ref[:, lanes] = _rope(kh, cosf, sinf, lane).astype(BF16)


def _proj(x, cosf, sinf, p, layer, widths):
    n, d = x.shape
    gm_w, q_rank, kv_rank, ssd_inner, conv_dim = widths
    head_w = MLA_HEADS * LANES
    res = lambda a: _resident(a.shape, layer)
    outs = [(gm_w, BF16), (head_w, BF16), (head_w, BF16), (MLA_HEADS * MLA_V, BF16), (LANES, F32), (ssd_inner, BF16), (conv_dim, BF16)]
    return pl.pallas_call(
        functools.partial(_proj_kernel, widths=widths),
        grid=(n // ROW_TILE,),
        in_specs=[_rows(d), _rows(LANES), _rows(LANES), res(p["mix_norm"]), res(p["w_proj"]),
                  res(p["gm_gain"]), res(p["gm_w"]), res(p["gm_b"]),
                  res(p["q_norm"]), res(p["kv_norm"]), res(p["w_uq"]), res(p["w_uk"]), res(p["w_uv"]),
                  res(p["q_gain"]), res(p["k_gain"])],
        out_specs=[_rows(w) for w, _ in outs],
        out_shape=[jax.ShapeDtypeStruct((n, w), dt) for w, dt in outs],
        compiler_params=_params("parallel"),
        name="proj",
    )(x, cosf, sinf, p["mix_norm"], p["w_proj"], p["gm_gain"], p["gm_w"], p["gm_b"],
      p["q_norm"], p["kv_norm"], p["w_uq"], p["w_uk"], p["w_uv"], p["q_gain"], p["k_gain"])


def _attn_kernel(q_ref, k_ref, v_ref, o_ref):
    tq = q_ref.shape[0]
    tk = ATTN_TK
    qi = pl.program_id(2)
    q = [q_ref[:, hh * LANES:(hh + 1) * LANES] for hh in range(2)]

    def step(kj, carry, masked):
        start = pl.multiple_of(kj * tk, tk)
        ks = k_ref[pl.ds(start, tk), :]
        vs = v_ref[pl.ds(start, tk), :]
        out = []
        for hh in range(2):
            m, l, acc = carry[hh]
            s = lax.dot_general(q[hh], ks[:, hh * LANES:(hh + 1) * LANES], (((1,), (1,)), ((), ())),
                                preferred_element_type=F32)
            if masked:
                qpos = qi * tq + lax.broadcasted_iota(jnp.int32, (tq, tk), 0)
                kpos = kj * tk + lax.broadcasted_iota(jnp.int32, (tq, tk), 1)
                s = jnp.where(kpos <= qpos, s, -jnp.inf)
            m_new = jnp.maximum(m, jnp.max(s, axis=-1, keepdims=True))
            alpha = jnp.exp(m - m_new)
            p = jnp.exp(s - m_new)
            l = alpha * l + jnp.sum(p, axis=-1, keepdims=True)
            acc = alpha * acc + jnp.dot(p.astype(BF16), vs, preferred_element_type=F32)
            out.append((m_new, l, acc))
        return tuple(out)

    init = tuple((jnp.full((tq, 1), -jnp.inf, F32), jnp.zeros((tq, 1), F32), jnp.zeros((tq, LANES), F32)) for _ in range(2))
    n_full = (qi * tq) // tk
    carry = lax.fori_loop(0, n_full, lambda kj, c: step(kj, c, False), init)
    for kj in range(tq // tk):
        carry = step(n_full + kj, carry, True)
    (_, l0, acc0), (_, l1, acc1) = carry
    lane = _lane_iota((tq, LANES))
    o_ref[...] = jnp.where(lane < MLA_V, acc0 / l0, acc1 / l1).astype(BF16)


def _attention(q, k, v):
    b, s, _ = q.shape
    pairs = MLA_HEADS // 2
    tq = min(ATTN_TQ, s)
    assert s % tq == 0 and tq % ATTN_TK == 0
    return pl.pallas_call(
        _attn_kernel,
        grid=(b, pairs, s // tq),
        in_specs=[pl.BlockSpec((None, tq, 2 * LANES), lambda bi, j, qi: (bi, qi, j)),
                  pl.BlockSpec((None, s, 2 * LANES), lambda bi, j, qi: (bi, 0, j)),
                  pl.BlockSpec((None, s, 2 * MLA_V), lambda bi, j, qi: (bi, 0, j))],
        out_specs=pl.BlockSpec((None, tq, 2 * MLA_V), lambda bi, j, qi: (bi, qi, j)),
        out_shape=jax.ShapeDtypeStruct((b, s, MLA_HEADS * MLA_V), BF16),
        compiler_params=_params("parallel", "parallel", "arbitrary"),
        name="attn",
    )(q, k, v)


def _cumsum_lanes(x):
    lane = _lane_iota(x.shape)
    sh = 1
    while sh < x.shape[-1]:
        x = x + jnp.where(lane >= sh, pltpu.roll(x, sh, 1), 0.0)
        sh *= 2
    return x


def _ssd_kernel(xbc_ref, z_ref, small_ref, conv_w_ref, conv_b_ref, dt_bias_ref, a_log_ref, d_skip_ref, gain_ref,
                y_ref, ext_ref, state_ref):
    tile = xbc_ref.shape[0]
    inner = z_ref.shape[1]
    gstate = SSD_GROUPS * SSD_STATE
    tail = 8

    @pl.when(pl.program_id(1) == 0)
    def _():
        ext_ref[0:tail, :] = jnp.zeros((tail, ext_ref.shape[1]), F32)
        state_ref[...] = jnp.zeros(state_ref.shape, F32)

    ext_ref[tail:tail + tile, :] = xbc_ref[...].astype(F32)
    conv = conv_b_ref[...]
    for kk in range(SSD_CONV):
        conv = conv + conv_w_ref[kk:kk + 1, :] * ext_ref[pl.ds(tail - (SSD_CONV - 1) + kk, tile), :]
    ext_ref[0:tail, :] = ext_ref[tile:tile + tail, :]
    xc = conv * jax.nn.sigmoid(conv)

    lane = _lane_iota((tile, LANES))
    head_lane = (lane >= DT_LANE0) & (lane < DT_LANE0 + SSD_HEADS)
    dt = jnp.where(head_lane, jax.nn.softplus(small_ref[...] + dt_bias_ref[...]), 0.0)
    da = dt * (-jnp.exp(a_log_ref[...]))

    sub = lax.broadcasted_iota(jnp.int32, (SSD_CHUNK, SSD_CHUNK), 0)
    lan = lax.broadcasted_iota(jnp.int32, (SSD_CHUNK, SSD_CHUNK), 1)
    causal = lan <= sub
    low_half = lan < SSD_HEAD_DIM
    top_half = sub < SSD_HEAD_DIM
    heads_per_group = SSD_HEADS // SSD_GROUPS

    for c in range(tile // SSD_CHUNK):
        rows = slice(c * SSD_CHUNK, (c + 1) * SSD_CHUNK)
        cs_row = _cumsum_lanes(da[rows].T)
        cs_col = cs_row.T
        dt_c = dt[rows]
        b_in = [xc[rows, inner + g * SSD_STATE:inner + (g + 1) * SSD_STATE].astype(BF16) for g in range(SSD_GROUPS)]
        c_in = [xc[rows, inner + gstate + g * SSD_STATE:inner + gstate + (g + 1) * SSD_STATE].astype(BF16)
                for g in range(SSD_GROUPS)]
        cb = [lax.dot_general(c_in[g], b_in[g], (((1,), (1,)), ((), ())), preferred_element_type=F32)
              for g in range(SSD_GROUPS)]
        for pr in range(SSD_HEADS // 2):
            g = (2 * pr) // heads_per_group
            lanes = slice(pr * LANES, (pr + 1) * LANES)
            xs = xc[rows, lanes]
            col = [cs_col[:, 2 * pr + i:2 * pr + i + 1] for i in range(2)]
            row = [cs_row[2 * pr + i:2 * pr + i + 1, :] for i in range(2)]
            dtc = [dt_c[:, 2 * pr + i:2 * pr + i + 1] for i in range(2)]
            tot = [cs_col[SSD_CHUNK - 1:SSD_CHUNK, 2 * pr + i:2 * pr + i + 1] for i in range(2)]
            xdt = xs * jnp.where(low_half, dtc[0], dtc[1])
            xdt_b = xdt.astype(BF16)
            y_d = []
            for i in range(2):
                lmat = jnp.exp(jnp.where(causal, col[i] - row[i], -jnp.inf))
                y_d.append(jnp.dot((cb[g] * lmat).astype(BF16), xdt_b, preferred_element_type=F32))
            y_diag = jnp.where(low_half, y_d[0], y_d[1])
            decay = jnp.where(low_half, jnp.exp(tot[0] - col[0]), jnp.exp(tot[1] - col[1]))
            states = lax.dot_general((xdt * decay).astype(BF16), b_in[g], (((0,), (0,)), ((), ())),
                                     preferred_element_type=F32)
            s_in = state_ref[pr]
            y_off = lax.dot_general(c_in[g], s_in.astype(BF16), (((1,), (1,)), ((), ())), preferred_element_type=F32)
            y_off = y_off * jnp.where(low_half, jnp.exp(col[0]), jnp.exp(col[1]))
            state_ref[pr] = s_in * jnp.where(top_half, jnp.exp(tot[0]), jnp.exp(tot[1])) + states
            y = y_diag + y_off + xs * d_skip_ref[:, lanes]
            zz = z_ref[rows, lanes].astype(F32)
            y_ref[rows, lanes] = y * (zz * jax.nn.sigmoid(zz))

    gw = inner // SSD_GROUPS
    for g in range(SSD_GROUPS):
        lanes = slice(g * gw, (g + 1) * gw)
        y_ref[:, lanes] = _rms(y_ref[:, lanes], gain_ref[:, lanes])


def _ssd(xbc, z, small, p, layer, b, s):
    n, conv_dim = xbc.shape
    inner = z.shape[1]
    tile = min(SSD_TILE, s)
    assert s % tile == 0 and tile % SSD_CHUNK == 0
    per_b = s // tile
    rows = lambda w: pl.BlockSpec((tile, w), lambda bi, si: (bi * per_b + si, 0))
    res = lambda a: _resident(a.shape, layer)
    return pl.pallas_call(
        _ssd_kernel,
        grid=(b, per_b),
        in_specs=[rows(conv_dim), rows(inner), rows(LANES), res(p["conv_w"]), res(p["conv_b"]), res(p["dt_bias"]),
                  res(p["a_log"]), res(p["d_skip"]), res(p["ssd_gain"])],
        out_specs=rows(inner),
        out_shape=jax.ShapeDtypeStruct((n, inner), F32),
        scratch_shapes=[pltpu.VMEM((tile + 8, conv_dim), F32),
                        pltpu.VMEM((SSD_HEADS // 2, 2 * SSD_HEAD_DIM, SSD_STATE), F32)],
        compiler_params=_params("parallel", "arbitrary"),
        name="ssd",
    )(xbc, z, small, p["conv_w"], p["conv_b"], p["dt_bias"], p["a_log"], p["d_skip"], p["ssd_gain"])


def _merge_kernel(x_ref, ya_ref, yb_ref, yc_ref, mix_gain_ref, w_gate_ref, w_branch_ref, w_out_ref, o_ref):
    x = x_ref[...]
    d = x.shape[1]
    h = _rms(x, mix_gain_ref[...]).astype(BF16)
    merged = None
    for i, y_ref in enumerate((ya_ref, yb_ref, yc_ref)):
        gate = jax.nn.sigmoid(jnp.dot(h, w_gate_ref[:, i * d:(i + 1) * d], preferred_element_type=F32))
        term = gate * jnp.dot(y_ref[...].astype(BF16), w_branch_ref[i], preferred_element_type=F32)
        merged = term if merged is None else merged + term
    o_ref[...] = x + jnp.dot(merged.astype(BF16), w_out_ref[...], preferred_element_type=F32)


def _merge(x, ya, yb, yc, p, layer):
    n, d = x.shape
    res = lambda a: _resident(a.shape, layer)
    return pl.pallas_call(
        _merge_kernel,
        grid=(n // ROW_TILE,),
        in_specs=[_rows(d), _rows(ya.shape[1]), _rows(yb.shape[1]), _rows(yc.shape[1]),
                  res(p["mix_norm"]), res(p["w_gate"]), res(p["w_branch"]), res(p["w_out"])],
        out_specs=_rows(d),
        out_shape=jax.ShapeDtypeStruct((n, d), F32),
        compiler_params=_params("parallel"),
        name="merge",
    )(x, ya, yb, yc, p["mix_norm"], p["w_gate"], p["w_branch"], p["w_out"])


def _row(a):
    return a[:, None, :]


def _pad_heads(a, used):
    lead = a.shape[:-1]
    a = a.reshape(lead + (-1, used))
    a = jnp.pad(a, [(0, 0)] * len(lead) + [(0, 0), (0, LANES - used)])
    return a.reshape(lead + (-1,))


def _pad_lanes(a, lane0):
    return _row(jnp.pad(a, ((0, 0), (lane0, LANES - lane0 - a.shape[1]))))


def kernel(x, positions, ffn1_norm, ffn1_w_in, ffn1_w_out, mix_norm, w_in, gm_v_norm, gm_w_s, gm_b_s, mla_q_norm, mla_kv_norm, mla_w_uq, mla_w_ukv, mla_q_gain, mla_k_gain, ssd_conv_w, ssd_conv_b, ssd_dt_bias, ssd_a_log, ssd_d, ssd_norm, w_branch, w_out, ffn2_norm, ffn2_w_in, ffn2_w_out):
    b, s, d = x.shape
    depth = w_in.shape[0]
    n = b * s
    gm_w = gm_v_norm.shape[1]
    q_rank = mla_q_norm.shape[1]
    kv_rank = mla_kv_norm.shape[1]
    ssd_inner = ssd_norm.shape[1]
    conv_dim = ssd_conv_b.shape[1]
    widths = (gm_w, q_rank, kv_rank, ssd_inner, conv_dim)

    sizes = (2 * gm_w, q_rank, kv_rank, MLA_ROPE, ssd_inner, conv_dim, SSD_HEADS, N_BRANCH * d)
    offs = [0]
    for w in sizes:
        offs.append(offs[-1] + w)
    col = lambda i: w_in[:, :, offs[i]:offs[i + 1]]
    w_small = jnp.zeros((depth, d, LANES), w_in.dtype)
    w_small = w_small.at[:, :, DT_LANE0:DT_LANE0 + SSD_HEADS].set(col(6))
    w_small = w_small.at[:, :, ROPE_LANE0:ROPE_LANE0 + MLA_ROPE].set(col(3))
    w_proj = jnp.concatenate([col(0), col(1), col(2), w_small, col(4), col(5)], axis=-1).astype(BF16)

    w_ukv = mla_w_ukv.reshape(depth, kv_rank, MLA_HEADS, MLA_NOPE + MLA_V)
    gains = lambda g_: jnp.tile(_pad_heads(g_, MLA_QK_DIM), (1, 1))[:, None, :]
    p = {
        "mix_norm": _row(mix_norm),
        "w_proj": w_proj,
        "w_gate": col(7).astype(BF16),
        "gm_gain": _row(gm_v_norm),
        "gm_w": gm_w_s,
        "gm_b": jnp.repeat(gm_b_s.transpose(0, 2, 1), gm_w // GM_GROUPS, axis=2),
        "q_norm": _row(mla_q_norm),
        "kv_norm": _row(mla_kv_norm),
        "w_uq": _pad_heads(mla_w_uq, MLA_QK_DIM).astype(BF16),
        "w_uk": _pad_heads(w_ukv[..., :MLA_NOPE].reshape(depth, kv_rank, -1), MLA_NOPE).astype(BF16),
        "w_uv": w_ukv[..., MLA_NOPE:].reshape(depth, kv_rank, -1).astype(BF16),
        "q_gain": gains(mla_q_gain * (MLA_QK_DIM ** -0.5)),
        "k_gain": gains(mla_k_gain),
        "conv_w": ssd_conv_w,
        "conv_b": _row(ssd_conv_b),
        "dt_bias": _pad_lanes(ssd_dt_bias, DT_LANE0),
        "a_log": _pad_lanes(ssd_a_log, DT_LANE0),
        "d_skip": _row(jnp.repeat(ssd_d, SSD_HEAD_DIM, axis=1)),
        "ssd_gain": _row(ssd_norm),
        "w_branch": w_branch.astype(BF16),
        "w_out": w_out.astype(BF16),
    }
    ffn = [(_row(ffn1_norm), ffn1_w_in.astype(BF16), ffn1_w_out.astype(BF16)),
           (_row(ffn2_norm), ffn2_w_in.astype(BF16), ffn2_w_out.astype(BF16))]

    cosf, sinf = _rope_tables(positions)
    xf = x.reshape(n, d)
    for layer in range(depth):
        xf = _ffn(xf, *ffn[0], layer)
        ya, q, k, v, small, z, xbc = _proj(xf, cosf, sinf, p, layer, widths)
        yb = _attention(q.reshape(b, s, -1), k.reshape(b, s, -1), v.reshape(b, s, -1)).reshape(n, -1)
        yc = _ssd(xbc, z, small, p, layer, b, s)
        xf = _merge(xf, ya, yb, yc, p, layer)
        xf = _ffn(xf, *ffn[1], layer)
    return xf.reshape(b, s, d)
```

```python
import functools

import jax
import jax.numpy as jnp
from jax import lax
from jax.experimental import pallas as pl
from jax.experimental.pallas import tpu as pltpu

F32 = jnp.float32
BF16 = jnp.bfloat16

LANES = 128
VMEM_LIMIT_BYTES = 56 * 1024 * 1024

EPS = 1e-6
FFN_RESID = 0.5
GM_GROUPS = 4
GM_CHUNK = 128
MLA_HEADS = 8
MLA_NOPE = 64
MLA_ROPE = 32
MLA_QK_DIM = MLA_NOPE + MLA_ROPE
MLA_V = 64
ROPE_THETA = 10000.0
LOG2_E = 1.4426950408889634
SSD_HEADS = 8
SSD_HEAD_DIM = 64
SSD_GROUPS = 2
SSD_STATE = 128
SSD_CONV = 4
SSD_CHUNK = 128
N_BRANCH = 3
DT_LANE0 = 0
ROPE_LANE0 = MLA_NOPE

ROW_TILE = 512
ATTN_KSUB = 128
ATTN_QSUB = 256
ATTN_AHEAD = 6
SSD_TILE = 512
FF_CHUNK = 256


def _rms(x, gain):
    ms = jnp.mean(x * x, axis=-1, keepdims=True)
    return x * lax.rsqrt(ms + EPS) * gain


def _gelu(x):
    return 0.5 * x * (1.0 + lax.erf(x * (2.0 ** -0.5)))


def _resident(shape, layer):
    zeros = (0,) * (len(shape) - 1)
    return pl.BlockSpec((None,) + tuple(shape[1:]), lambda *_: (layer,) + zeros, pipeline_mode=pl.Buffered(1))


def _rows(width, tile=ROW_TILE):
    return pl.BlockSpec((tile, width), lambda i: (i, 0))


def _params(*sem):
    return pltpu.CompilerParams(dimension_semantics=sem, vmem_limit_bytes=VMEM_LIMIT_BYTES)


def _rope_table_kernel(pos_ref, freq_ref, cos_ref, sin_ref):
    ang = freq_ref[...] * pos_ref[...].astype(F32)
    cos_ref[...] = jnp.cos(ang)
    sin_ref[...] = jnp.sin(ang)


def _rope_tables(positions):
    n = positions.size
    half = MLA_ROPE // 2
    inv_freq = 1.0 / (ROPE_THETA ** (jnp.arange(0, MLA_ROPE, 2, dtype=F32) / MLA_ROPE))
    tile = min(4096, n)
    return pl.pallas_call(
        _rope_table_kernel,
        grid=(n // tile,),
        in_specs=[pl.BlockSpec((1, tile), lambda i: (0, i)),
                  pl.BlockSpec((half, 1), lambda i: (0, 0))],
        out_specs=[pl.BlockSpec((half, tile), lambda i: (0, i))] * 2,
        out_shape=[jax.ShapeDtypeStruct((half, n), F32)] * 2,
        compiler_params=_params("parallel"),
        name="rope_tables",
    )(positions.reshape(1, n), inv_freq.reshape(half, 1))


def _ffn_body(x, gain, w_in_ref, w_out_ref, hid_ref):
    d_ff = w_out_ref.shape[0]
    h = _rms(x, gain).astype(BF16)
    for c in range(d_ff // FF_CHUNK):
        lo = c * FF_CHUNK
        g = jnp.dot(h, w_in_ref[:, lo:lo + FF_CHUNK], preferred_element_type=F32)
        u = jnp.dot(h, w_in_ref[:, d_ff + lo:d_ff + lo + FF_CHUNK], preferred_element_type=F32)
        hid_ref[:, lo:lo + FF_CHUNK] = (g * jax.nn.sigmoid(g) * u).astype(BF16)
    y = jnp.dot(hid_ref[...], w_out_ref[...], preferred_element_type=F32)
    return x + FFN_RESID * y


def _ffn_kernel(x_ref, gain_ref, w_in_ref, w_out_ref, o_ref, hid_ref):
    o_ref[...] = _ffn_body(x_ref[...], gain_ref[...], w_in_ref, w_out_ref, hid_ref)


def _ffn(x, gain, w_in, w_out, layer):
    n, d = x.shape
    d_ff = w_out.shape[1]
    assert d_ff % FF_CHUNK == 0 and n % ROW_TILE == 0
    return pl.pallas_call(
        _ffn_kernel,
        grid=(n // ROW_TILE,),
        in_specs=[_rows(d), _resident(gain.shape, layer), _resident(w_in.shape, layer), _resident(w_out.shape, layer)],
        out_specs=_rows(d),
        out_shape=jax.ShapeDtypeStruct((n, d), F32),
        scratch_shapes=[pltpu.VMEM((ROW_TILE, d_ff), BF16)],
        compiler_params=_params("parallel"),
        name="ffn",
    )(x, gain, w_in, w_out)


def _lane_iota(shape):
    return lax.broadcasted_iota(jnp.int32, shape, len(shape) - 1)


def _head_norm_rope(x, gain, cos_t, sin_t):
    half = MLA_ROPE // 2
    ms = jnp.sum(x * x, axis=0, keepdims=True) * (1.0 / MLA_QK_DIM)
    y = x * lax.rsqrt(ms + EPS) * gain
    x1 = y[MLA_NOPE:MLA_NOPE + half]
    x2 = y[MLA_NOPE + half:MLA_QK_DIM]
    pad = jnp.zeros((LANES - MLA_QK_DIM, x.shape[1]), F32)
    return jnp.concatenate([y[:MLA_NOPE], x1 * cos_t - x2 * sin_t, x2 * cos_t + x1 * sin_t, pad], axis=0)


def _rms_rows(x, gain):
    ms = jnp.mean(x * x, axis=0, keepdims=True)
    return x * lax.rsqrt(ms + EPS) * gain


def _proj_kernel(x_ref, cos_ref, sin_ref, mix_gain_ref, w_proj_ref,
                 gm_gain_ref, gm_w_ref, gm_b_ref,
                 q_norm_ref, kv_norm_ref, w_uq_ref, w_uk_ref, w_uv_ref, q_gain_ref, k_gain_ref,
                 ya_ref, qt_ref, k_ref, vt_ref, small_ref, z_ref, xbc_ref,
                 *, widths):
    gm_w, q_rank, kv_rank, ssd_inner, conv_dim = widths
    tile = x_ref.shape[0]
    h = _rms(x_ref[...], mix_gain_ref[...]).astype(BF16)

    def proj(lo, width):
        return jnp.dot(h, w_proj_ref[:, lo:lo + width], preferred_element_type=F32)

    off_uv = 0
    off_cq = off_uv + 2 * gm_w
    off_ckv = off_cq + q_rank
    off_small = off_ckv + kv_rank
    off_z = off_small + LANES
    off_xbc = off_z + ssd_inner

    small = proj(off_small, LANES)
    small_ref[...] = small
    z_ref[...] = proj(off_z, ssd_inner).astype(BF16)
    xbc_ref[...] = proj(off_xbc, conv_dim).astype(BF16)

    u = _gelu(proj(off_uv, gm_w))
    v = _gelu(proj(off_uv + gm_w, gm_w))
    v = _rms(v, gm_gain_ref[...]).astype(BF16)
    n_chunk = tile // GM_CHUNK
    gdim = gm_w // GM_GROUPS
    row = lax.broadcasted_iota(jnp.int32, (GM_CHUNK, GM_CHUNK), 0)
    col = lax.broadcasted_iota(jnp.int32, (GM_CHUNK, GM_CHUNK), 1)
    for g in range(GM_GROUPS):
        w = jnp.where(col <= row, gm_w_ref[g], 0.0).astype(BF16)
        rhs = jnp.concatenate([v[c * GM_CHUNK:(c + 1) * GM_CHUNK, g * gdim:(g + 1) * gdim] for c in range(n_chunk)], axis=1)
        sp = jnp.dot(w, rhs, preferred_element_type=F32)
        bias = gm_b_ref[:, g * gdim:(g + 1) * gdim]
        for c in range(n_chunk):
            rows = slice(c * GM_CHUNK, (c + 1) * GM_CHUNK)
            ya_ref[rows, g * gdim:(g + 1) * gdim] = (
                u[rows, g * gdim:(g + 1) * gdim] * (sp[:, c * gdim:(c + 1) * gdim] + bias)).astype(BF16)

    lat_t = jnp.concatenate([proj(off_cq, q_rank), proj(off_ckv, kv_rank), small], axis=1).T
    qn_t = _rms_rows(lat_t[:q_rank], q_norm_ref[...]).astype(BF16)
    kvn_t = _rms_rows(lat_t[q_rank:q_rank + kv_rank], kv_norm_ref[...]).astype(BF16)
    k_rope_t = lat_t[q_rank + kv_rank + ROPE_LANE0:q_rank + kv_rank + ROPE_LANE0 + MLA_ROPE]
    cos_t = cos_ref[...]
    sin_t = sin_ref[...]
    vt_ref[...] = jnp.dot(w_uv_ref[...], kvn_t, preferred_element_type=F32).astype(BF16)
    q_all = jnp.dot(w_uq_ref[...], qn_t, preferred_element_type=F32)
    k_nope = jnp.dot(w_uk_ref[...], kvn_t, preferred_element_type=F32)
    q_gain = q_gain_ref[...]
    k_gain = k_gain_ref[...]
    for hd in range(MLA_HEADS):
        qh = q_all[hd * MLA_QK_DIM:(hd + 1) * MLA_QK_DIM]
        qt_ref[hd * LANES:(hd + 1) * LANES, :] = _head_norm_rope(qh, q_gain, cos_t, sin_t).astype(BF16)
        kh = jnp.concatenate([k_nope[hd * MLA_NOPE:(hd + 1) * MLA_NOPE], k_rope_t], axis=0)
        k_ref[:, hd * LANES:(hd + 1) * LANES] = _head_norm_rope(kh, k_gain, cos_t, sin_t).T.astype(BF16)


def _proj(x, cosf, sinf, p, layer, widths):
    n, d = x.shape
    gm_w, q_rank, kv_rank, ssd_inner, conv_dim = widths
    head_w = MLA_HEADS * LANES
    v_w = MLA_HEADS * MLA_V
    tiles = n // ROW_TILE
    res = lambda a: _resident(a.shape, layer)
    table = pl.BlockSpec((MLA_ROPE // 2, ROW_TILE), lambda i: (0, i))
    chan_major = lambda c: pl.BlockSpec((None, c, ROW_TILE), lambda i: (i, 0, 0))
    return pl.pallas_call(
        functools.partial(_proj_kernel, widths=widths),
        grid=(tiles,),
        in_specs=[_rows(d), table, table, res(p["mix_norm"]), res(p["w_proj"]),
                  res(p["gm_gain"]), res(p["gm_w"]), res(p["gm_b"]),
                  res(p["q_norm"]), res(p["kv_norm"]), res(p["w_uq"]), res(p["w_uk"]), res(p["w_uv"]),
                  res(p["q_gain"]), res(p["k_gain"])],
        out_specs=[_rows(gm_w), chan_major(head_w), _rows(head_w), chan_major(v_w),
                   _rows(LANES), _rows(ssd_inner), _rows(conv_dim)],
        out_shape=[jax.ShapeDtypeStruct((n, gm_w), BF16),
                   jax.ShapeDtypeStruct((tiles, head_w, ROW_TILE), BF16),
                   jax.ShapeDtypeStruct((n, head_w), BF16),
                   jax.ShapeDtypeStruct((tiles, v_w, ROW_TILE), BF16),
                   jax.ShapeDtypeStruct((n, LANES), F32),
                   jax.ShapeDtypeStruct((n, ssd_inner), BF16),
                   jax.ShapeDtypeStruct((n, conv_dim), BF16)],
        compiler_params=_params("parallel"),
        name="proj",
    )(x, cosf, sinf, p["mix_norm"], p["w_proj"], p["gm_gain"], p["gm_w"], p["gm_b"],
      p["q_norm"], p["kv_norm"], p["w_uq"], p["w_uk"], p["w_uv"], p["q_gain"], p["k_gain"])


def _attn_kernel(qt_ref, k_ref, vt_ref, o_ref, m_ref, l_ref, acc_ref):
    tq = qt_ref.shape[1]
    tk = k_ref.shape[1]
    qi = pl.program_id(2)

    m_ref[...] = jnp.full(m_ref.shape, -jnp.inf, F32)
    l_ref[...] = jnp.zeros(l_ref.shape, F32)
    acc_ref[...] = jnp.zeros(acc_ref.shape, F32)

    def sub_tiles(diagonal):
        out = []
        for k0 in range(0, tk, ATTN_KSUB):
            for hh in range(2):
                for q0 in range(0, tq, ATTN_QSUB):
                    if diagonal and k0 > q0 + ATTN_QSUB - 1:
                        continue
                    out.append((hh, k0, q0, diagonal and k0 + ATTN_KSUB - 1 > q0))
        return out

    def scores(kj, hh, k0, q0):
        return jnp.dot(k_ref[kj, k0:k0 + ATTN_KSUB, hh * LANES:(hh + 1) * LANES],
                       qt_ref[hh * LANES:(hh + 1) * LANES, q0:q0 + ATTN_QSUB], preferred_element_type=F32)

    def absorb(kj, s_next, diagonal):
        todo = sub_tiles(diagonal)
        ahead = list(s_next)
        for i, (hh, k0, q0, masked) in enumerate(todo):
            s = ahead.pop(0)
            if i + ATTN_AHEAD < len(todo):
                ahead.append(scores(kj, *todo[i + ATTN_AHEAD][:3]))
            elif not diagonal:
                ahead.append(scores(kj + 1, *todo[i + ATTN_AHEAD - len(todo)][:3]))
            qcols = slice(q0, q0 + ATTN_QSUB)
            if masked:
                kpos = k0 + lax.broadcasted_iota(jnp.int32, s.shape, 0)
                qpos = q0 + lax.broadcasted_iota(jnp.int32, s.shape, 1)
                s = jnp.where(kpos <= qpos, s, -jnp.inf)
            m = m_ref[hh, :, qcols]
            m_new = jnp.maximum(m, jnp.max(s, axis=0, keepdims=True))
            alpha = jnp.exp2(m - m_new)
            p = jnp.exp2(s - m_new)
            m_ref[hh, :, qcols] = m_new
            l_ref[hh, :, qcols] = alpha * l_ref[hh, :, qcols] + jnp.sum(p, axis=0, keepdims=True)
            pv = jnp.dot(vt_ref[kj, hh * MLA_V:(hh + 1) * MLA_V, k0:k0 + ATTN_KSUB], p.astype(BF16),
                         preferred_element_type=F32)
            acc_ref[hh, :, qcols] = alpha * acc_ref[hh, :, qcols] + pv
        return tuple(ahead)

    assert [t[:3] for t in sub_tiles(True)[:ATTN_AHEAD]] == [t[:3] for t in sub_tiles(False)[:ATTN_AHEAD]]
    first = tuple(scores(0, *t[:3]) for t in sub_tiles(False)[:ATTN_AHEAD])
    s_diag = lax.fori_loop(0, qi, lambda kj, s: absorb(kj, s, False), first)
    absorb(qi, s_diag, True)
    out = jnp.concatenate([acc_ref[hh] / l_ref[hh] for hh in range(2)], axis=0)
    o_ref[...] = out.T.astype(BF16)


def _attention(qt, k, vt, b, s):
    tiles, _, t = qt.shape
    n = k.shape[0]
    pairs = MLA_HEADS // 2
    per_b = s // t
    assert tiles == b * per_b
    return pl.pallas_call(
        _attn_kernel,
        grid=(b, pairs, per_b),
        in_specs=[pl.BlockSpec((None, 2 * LANES, t), lambda bi, j, qi: (bi * per_b + qi, j, 0)),
                  pl.BlockSpec((per_b, t, 2 * LANES), lambda bi, j, qi: (bi, 0, j)),
                  pl.BlockSpec((per_b, 2 * MLA_V, t), lambda bi, j, qi: (bi, j, 0))],
        out_specs=pl.BlockSpec((t, 2 * MLA_V), lambda bi, j, qi: (bi * per_b + qi, j)),
        out_shape=jax.ShapeDtypeStruct((n, MLA_HEADS * MLA_V), BF16),
        scratch_shapes=[pltpu.VMEM((2, 1, t), F32), pltpu.VMEM((2, 1, t), F32), pltpu.VMEM((2, MLA_V, t), F32)],
        compiler_params=_params("parallel", "parallel", "arbitrary"),
        name="attn",
    )(qt, k.reshape(tiles, t, k.shape[1]), vt)


def _cumsum_lanes(x):
    lane = _lane_iota(x.shape)
    sh = 1
    while sh < x.shape[-1]:
        x = x + jnp.where(lane >= sh, pltpu.roll(x, sh, 1), 0.0)
        sh *= 2
    return x


def _ssd_kernel(xbc_ref, z_ref, small_ref, conv_w_ref, conv_b_ref, dt_bias_ref, a_log_ref, d_skip_ref, gain_ref,
                y_ref, ext_ref, state_ref):
    tile = xbc_ref.shape[0]
    inner = z_ref.shape[1]
    gstate = SSD_GROUPS * SSD_STATE

    @pl.when(pl.program_id(1) == 0)
    def _():
        ext_ref[0:SSD_CHUNK, :] = jnp.zeros((SSD_CHUNK, ext_ref.shape[1]), BF16)
        state_ref[...] = jnp.zeros(state_ref.shape, F32)

    ext_ref[SSD_CHUNK:SSD_CHUNK + tile, :] = xbc_ref[...]

    n_shift = SSD_CONV - 1
    srow = lax.broadcasted_iota(jnp.int32, (SSD_CHUNK, 2 * SSD_CHUNK), 0)
    scol = lax.broadcasted_iota(jnp.int32, (SSD_CHUNK, 2 * SSD_CHUNK), 1)
    pick = jnp.concatenate([jnp.where(scol == SSD_CHUNK + srow - (d + 1), 1.0, 0.0).astype(BF16)
                            for d in range(n_shift)], axis=0)

    def conv_chunk(c):
        window = ext_ref[c * SSD_CHUNK:(c + 2) * SSD_CHUNK, :]
        shifted = jnp.dot(pick, window, preferred_element_type=F32)
        conv = conv_b_ref[...] + conv_w_ref[n_shift:n_shift + 1, :] * window[SSD_CHUNK:].astype(F32)
        for d in range(n_shift):
            conv = conv + conv_w_ref[n_shift - 1 - d:n_shift - d, :] * shifted[d * SSD_CHUNK:(d + 1) * SSD_CHUNK]
        return conv * jax.nn.sigmoid(conv)

    lane = _lane_iota((tile, LANES))
    head_lane = (lane >= DT_LANE0) & (lane < DT_LANE0 + SSD_HEADS)
    dt = jnp.where(head_lane, jax.nn.softplus(small_ref[...] + dt_bias_ref[...]), 0.0)
    da = dt * (-jnp.exp(a_log_ref[...]))

    sub = lax.broadcasted_iota(jnp.int32, (SSD_CHUNK, SSD_CHUNK), 0)
    lan = lax.broadcasted_iota(jnp.int32, (SSD_CHUNK, SSD_CHUNK), 1)
    causal = lan <= sub
    low_half = lan < SSD_HEAD_DIM
    top_half = sub < SSD_HEAD_DIM
    heads_per_group = SSD_HEADS // SSD_GROUPS

    for c in range(tile // SSD_CHUNK):
        rows = slice(c * SSD_CHUNK, (c + 1) * SSD_CHUNK)
        cs_row = _cumsum_lanes(da[rows].T[DT_LANE0:DT_LANE0 + SSD_HEADS])
        cs_col = jnp.concatenate([cs_row, jnp.zeros((SSD_CHUNK - SSD_HEADS, SSD_CHUNK), F32)], axis=0).T
        dt_c = dt[rows]
        xc = conv_chunk(c)
        b_in = [xc[:, inner + g * SSD_STATE:inner + (g + 1) * SSD_STATE].astype(BF16) for g in range(SSD_GROUPS)]
        c_in = [xc[:, inner + gstate + g * SSD_STATE:inner + gstate + (g + 1) * SSD_STATE].astype(BF16)
                for g in range(SSD_GROUPS)]
        cb = [lax.dot_general(c_in[g], b_in[g], (((1,), (1,)), ((), ())), preferred_element_type=F32)
              for g in range(SSD_GROUPS)]
        for pr in range(SSD_HEADS // 2):
            g = (2 * pr) // heads_per_group
            lanes = slice(pr * LANES, (pr + 1) * LANES)
            xs = xc[:, lanes]
            col = [cs_col[:, 2 * pr + i:2 * pr + i + 1] for i in range(2)]
            row = [cs_row[2 * pr + i:2 * pr + i + 1, :] for i in range(2)]
            dtc = [dt_c[:, 2 * pr + i:2 * pr + i + 1] for i in range(2)]
            tot = [cs_col[SSD_CHUNK - 1:SSD_CHUNK, 2 * pr + i:2 * pr + i + 1] for i in range(2)]
            xdt = xs * jnp.where(low_half, dtc[0], dtc[1])
            xdt_b = xdt.astype(BF16)
            y_d = []
            for i in range(2):
                lmat = jnp.exp(jnp.where(causal, col[i] - row[i], -jnp.inf))
                y_d.append(jnp.dot((cb[g] * lmat).astype(BF16), xdt_b, preferred_element_type=F32))
            y_diag = jnp.where(low_half, y_d[0], y_d[1])
            decay = jnp.where(low_half, jnp.exp(tot[0] - col[0]), jnp.exp(tot[1] - col[1]))
            states = lax.dot_general((xdt * decay).astype(BF16), b_in[g], (((0,), (0,)), ((), ())),
                                     preferred_element_type=F32)
            s_in = state_ref[pr]
            y_off = lax.dot_general(c_in[g], s_in.astype(BF16), (((1,), (1,)), ((), ())), preferred_element_type=F32)
            y_off = y_off * jnp.where(low_half, jnp.exp(col[0]), jnp.exp(col[1]))
            state_ref[pr] = s_in * jnp.where(top_half, jnp.exp(tot[0]), jnp.exp(tot[1])) + states
            y = y_diag + y_off + xs * d_skip_ref[:, lanes]
            zz = z_ref[rows, lanes].astype(F32)
            y_ref[rows, lanes] = y * (zz * jax.nn.sigmoid(zz))

    ext_ref[0:SSD_CHUNK, :] = ext_ref[tile:tile + SSD_CHUNK, :]

    gw = inner // SSD_GROUPS
    for g in range(SSD_GROUPS):
        lanes = slice(g * gw, (g + 1) * gw)
        y_ref[:, lanes] = _rms(y_ref[:, lanes], gain_ref[:, lanes])


def _ssd(xbc, z, small, p, layer, b, s):
    n, conv_dim = xbc.shape
    inner = z.shape[1]
    tile = min(SSD_TILE, s)
    assert s % tile == 0 and tile % SSD_CHUNK == 0
    per_b = s // tile
    rows = lambda w: pl.BlockSpec((tile, w), lambda bi, si: (bi * per_b + si, 0))
    res = lambda a: _resident(a.shape, layer)
    return pl.pallas_call(
        _ssd_kernel,
        grid=(b, per_b),
        in_specs=[rows(conv_dim), rows(inner), rows(LANES), res(p["conv_w"]), res(p["conv_b"]), res(p["dt_bias"]),
                  res(p["a_log"]), res(p["d_skip"]), res(p["ssd_gain"])],
        out_specs=rows(inner),
        out_shape=jax.ShapeDtypeStruct((n, inner), F32),
        scratch_shapes=[pltpu.VMEM((SSD_CHUNK + tile, conv_dim), BF16),
                        pltpu.VMEM((SSD_HEADS // 2, 2 * SSD_HEAD_DIM, SSD_STATE), F32)],
        compiler_params=_params("parallel", "arbitrary"),
        name="ssd",
    )(xbc, z, small, p["conv_w"], p["conv_b"], p["dt_bias"], p["a_log"], p["d_skip"], p["ssd_gain"])


def _merge_kernel(x_ref, ya_ref, yb_ref, yc_ref, mix_gain_ref, w_gate_ref, w_branch_ref, w_out_ref, o_ref):
    x = x_ref[...]
    d = x.shape[1]
    h = _rms(x, mix_gain_ref[...]).astype(BF16)
    merged = None
    for i, y_ref in enumerate((ya_ref, yb_ref, yc_ref)):
        gate = jax.nn.sigmoid(jnp.dot(h, w_gate_ref[:, i * d:(i + 1) * d], preferred_element_type=F32))
        term = gate * jnp.dot(y_ref[...].astype(BF16), w_branch_ref[i], preferred_element_type=F32)
        merged = term if merged is None else merged + term
    o_ref[...] = x + jnp.dot(merged.astype(BF16), w_out_ref[...], preferred_element_type=F32)


def _merge(x, ya, yb, yc, p, layer):
    n, d = x.shape
    res = lambda a: _resident(a.shape, layer)
    return pl.pallas_call(
        _merge_kernel,
        grid=(n // ROW_TILE,),
        in_specs=[_rows(d), _rows(ya.shape[1]), _rows(yb.shape[1]), _rows(yc.shape[1]),
                  res(p["mix_norm"]), res(p["w_gate"]), res(p["w_branch"]), res(p["w_out"])],
        out_specs=_rows(d),
        out_shape=jax.ShapeDtypeStruct((n, d), F32),
        compiler_params=_params("parallel"),
        name="merge",
    )(x, ya, yb, yc, p["mix_norm"], p["w_gate"], p["w_branch"], p["w_out"])


def _row(a):
    return a[:, None, :]


def _col(a):
    return a[:, :, None]


def _pad_lanes(a, lane0):
    return _row(jnp.pad(a, ((0, 0), (lane0, LANES - lane0 - a.shape[1]))))


def kernel(x, positions, ffn1_norm, ffn1_w_in, ffn1_w_out, mix_norm, w_in, gm_v_norm, gm_w_s, gm_b_s, mla_q_norm, mla_kv_norm, mla_w_uq, mla_w_ukv, mla_q_gain, mla_k_gain, ssd_conv_w, ssd_conv_b, ssd_dt_bias, ssd_a_log, ssd_d, ssd_norm, w_branch, w_out, ffn2_norm, ffn2_w_in, ffn2_w_out):
    b, s, d = x.shape
    depth = w_in.shape[0]
    n = b * s
    gm_w = gm_v_norm.shape[1]
    q_rank = mla_q_norm.shape[1]
    kv_rank = mla_kv_norm.shape[1]
    ssd_inner = ssd_norm.shape[1]
    conv_dim = ssd_conv_b.shape[1]
    widths = (gm_w, q_rank, kv_rank, ssd_inner, conv_dim)

    sizes = (2 * gm_w, q_rank, kv_rank, MLA_ROPE, ssd_inner, conv_dim, SSD_HEADS, N_BRANCH * d)
    offs = [0]
    for w in sizes:
        offs.append(offs[-1] + w)
    col = lambda i: w_in[:, :, offs[i]:offs[i + 1]]
    w_small = jnp.zeros((depth, d, LANES), w_in.dtype)
    w_small = w_small.at[:, :, DT_LANE0:DT_LANE0 + SSD_HEADS].set(col(6))
    w_small = w_small.at[:, :, ROPE_LANE0:ROPE_LANE0 + MLA_ROPE].set(col(3))
    w_proj = jnp.concatenate([col(0), col(1), col(2), w_small, col(4), col(5)], axis=-1).astype(BF16)

    w_ukv = mla_w_ukv.reshape(depth, kv_rank, MLA_HEADS, MLA_NOPE + MLA_V)
    out_major = lambda w: w.reshape(depth, kv_rank, -1).transpose(0, 2, 1).astype(BF16)
    p = {
        "mix_norm": _row(mix_norm),
        "w_proj": w_proj,
        "w_gate": col(7).astype(BF16),
        "gm_gain": _row(gm_v_norm),
        "gm_w": gm_w_s,
        "gm_b": jnp.repeat(gm_b_s.transpose(0, 2, 1), gm_w // GM_GROUPS, axis=2),
        "q_norm": _col(mla_q_norm),
        "kv_norm": _col(mla_kv_norm),
        "w_uq": mla_w_uq.transpose(0, 2, 1).astype(BF16),
        "w_uk": out_major(w_ukv[..., :MLA_NOPE]),
        "w_uv": out_major(w_ukv[..., MLA_NOPE:]),
        "q_gain": _col(mla_q_gain * (MLA_QK_DIM ** -0.5 * LOG2_E)),
        "k_gain": _col(mla_k_gain),
        "conv_w": ssd_conv_w,
        "conv_b": _row(ssd_conv_b),
        "dt_bias": _pad_lanes(ssd_dt_bias, DT_LANE0),
        "a_log": _pad_lanes(ssd_a_log, DT_LANE0),
        "d_skip": _row(jnp.repeat(ssd_d, SSD_HEAD_DIM, axis=1)),
        "ssd_gain": _row(ssd_norm),
        "w_branch": w_branch.astype(BF16),
        "w_out": w_out.astype(BF16),
    }
    ffn = [(_row(ffn1_norm), ffn1_w_in.astype(BF16), ffn1_w_out.astype(BF16)),
           (_row(ffn2_norm), ffn2_w_in.astype(BF16), ffn2_w_out.astype(BF16))]

    cosf, sinf = _rope_tables(positions)
    xf = x.reshape(n, d)
    for layer in range(depth):
        xf = _ffn(xf, *ffn[0], layer)
        ya, qt, k, vt, small, z, xbc = _proj(xf, cosf, sinf, p, layer, widths)
        yb = _attention(qt, k, vt, b, s)
        yc = _ssd(xbc, z, small, p, layer, b, s)
        xf = _merge(xf, ya, yb, yc, p, layer)
        xf = _ffn(xf, *ffn[1], layer)
    return xf.reshape(b, s, d)
```

```python
import functools

import jax
import jax.numpy as jnp
from jax import lax
from jax.experimental import pallas as pl
from jax.experimental.pallas import tpu as pltpu

F32 = jnp.float32
BF16 = jnp.bfloat16

LANES = 128
VMEM_LIMIT_BYTES = 56 * 1024 * 1024

EPS = 1e-6
FFN_RESID = 0.5
GM_GROUPS = 4
GM_CHUNK = 128
MLA_HEADS = 8
MLA_NOPE = 64
MLA_ROPE = 32
MLA_QK_DIM = MLA_NOPE + MLA_ROPE
MLA_V = 64
ROPE_THETA = 10000.0
LOG2_E = 1.4426950408889634
SSD_HEADS = 8
SSD_HEAD_DIM = 64
SSD_GROUPS = 2
SSD_STATE = 128
SSD_CONV = 4
SSD_CHUNK = 128
N_BRANCH = 3
DT_LANE0 = 0
ROPE_LANE0 = MLA_NOPE

ROW_TILE = 512
ATTN_KSUB = 128
ATTN_QSUB = 256
ATTN_AHEAD = 6
ATTN_SUM_ROWS = 16
ATTN_Q_TILES = 2
SSD_TILE = 512
FF_CHUNK = 256


def _rms(x, gain):
    ms = jnp.mean(x * x, axis=-1, keepdims=True)
    return x * lax.rsqrt(ms + EPS) * gain


def _gelu(x):
    return 0.5 * x * (1.0 + lax.erf(x * (2.0 ** -0.5)))


def _resident(shape, layer):
    zeros = (0,) * (len(shape) - 1)
    return pl.BlockSpec((None,) + tuple(shape[1:]), lambda *_: (layer,) + zeros, pipeline_mode=pl.Buffered(1))


def _rows(width, tile=ROW_TILE):
    return pl.BlockSpec((tile, width), lambda i: (i, 0))


def _params(*sem):
    return pltpu.CompilerParams(dimension_semantics=sem, vmem_limit_bytes=VMEM_LIMIT_BYTES)


def _rope_table_kernel(pos_ref, freq_ref, cos_ref, sin_ref):
    ang = freq_ref[...] * pos_ref[...].astype(F32)
    cos_ref[...] = jnp.cos(ang)
    sin_ref[...] = jnp.sin(ang)


def _rope_tables(positions):
    n = positions.size
    half = MLA_ROPE // 2
    inv_freq = 1.0 / (ROPE_THETA ** (jnp.arange(0, MLA_ROPE, 2, dtype=F32) / MLA_ROPE))
    tile = min(4096, n)
    return pl.pallas_call(
        _rope_table_kernel,
        grid=(n // tile,),
        in_specs=[pl.BlockSpec((1, tile), lambda i: (0, i)),
                  pl.BlockSpec((half, 1), lambda i: (0, 0))],
        out_specs=[pl.BlockSpec((half, tile), lambda i: (0, i))] * 2,
        out_shape=[jax.ShapeDtypeStruct((half, n), F32)] * 2,
        compiler_params=_params("parallel"),
        name="rope_tables",
    )(positions.reshape(1, n), inv_freq.reshape(half, 1))


def _ffn_body(x, gain, w_in_ref, w_out_ref, hid_ref):
    d_ff = w_out_ref.shape[0]
    h = _rms(x, gain).astype(BF16)
    for c in range(d_ff // FF_CHUNK):
        lo = c * FF_CHUNK
        g = jnp.dot(h, w_in_ref[:, lo:lo + FF_CHUNK], preferred_element_type=F32)
        u = jnp.dot(h, w_in_ref[:, d_ff + lo:d_ff + lo + FF_CHUNK], preferred_element_type=F32)
        hid_ref[:, lo:lo + FF_CHUNK] = (g * jax.nn.sigmoid(g) * u).astype(BF16)
    y = jnp.dot(hid_ref[...], w_out_ref[...], preferred_element_type=F32)
    return x + FFN_RESID * y


def _ffn_kernel(x_ref, gain_ref, w_in_ref, w_out_ref, o_ref, hid_ref):
    o_ref[...] = _ffn_body(x_ref[...], gain_ref[...], w_in_ref, w_out_ref, hid_ref)


def _ffn(x, gain, w_in, w_out, layer):
    n, d = x.shape
    d_ff = w_out.shape[1]
    assert d_ff % FF_CHUNK == 0 and n % ROW_TILE == 0
    return pl.pallas_call(
        _ffn_kernel,
        grid=(n // ROW_TILE,),
        in_specs=[_rows(d), _resident(gain.shape, layer), _resident(w_in.shape, layer), _resident(w_out.shape, layer)],
        out_specs=_rows(d),
        out_shape=jax.ShapeDtypeStruct((n, d), F32),
        scratch_shapes=[pltpu.VMEM((ROW_TILE, d_ff), BF16)],
        compiler_params=_params("parallel"),
        name="ffn",
    )(x, gain, w_in, w_out)


def _lane_iota(shape):
    return lax.broadcasted_iota(jnp.int32, shape, len(shape) - 1)


def _head_norm_rope(x, gain, cos_t, sin_t):
    half = MLA_ROPE // 2
    ms = jnp.sum(x * x, axis=0, keepdims=True) * (1.0 / MLA_QK_DIM)
    y = x * lax.rsqrt(ms + EPS) * gain
    x1 = y[MLA_NOPE:MLA_NOPE + half]
    x2 = y[MLA_NOPE + half:MLA_QK_DIM]
    pad = jnp.zeros((LANES - MLA_QK_DIM, x.shape[1]), F32)
    return jnp.concatenate([y[:MLA_NOPE], x1 * cos_t - x2 * sin_t, x2 * cos_t + x1 * sin_t, pad], axis=0)


def _rms_rows(x, gain):
    ms = jnp.mean(x * x, axis=0, keepdims=True)
    return x * lax.rsqrt(ms + EPS) * gain


def _proj_kernel(x_ref, cos_ref, sin_ref, mix_gain_ref, w_proj_ref,
                 gm_gain_ref, gm_w_ref, gm_b_ref,
                 q_norm_ref, kv_norm_ref, w_uq_ref, w_uk_ref, w_uv_ref, q_gain_ref, k_gain_ref,
                 ya_ref, qt_ref, k_ref, vt_ref, small_ref, z_ref, xbc_ref,
                 *, widths):
    gm_w, q_rank, kv_rank, ssd_inner, conv_dim = widths
    tile = x_ref.shape[0]
    h = _rms(x_ref[...], mix_gain_ref[...]).astype(BF16)

    def proj(lo, width):
        return jnp.dot(h, w_proj_ref[:, lo:lo + width], preferred_element_type=F32)

    off_uv = 0
    off_cq = off_uv + 2 * gm_w
    off_ckv = off_cq + q_rank
    off_small = off_ckv + kv_rank
    off_z = off_small + LANES
    off_xbc = off_z + ssd_inner

    u_raw = proj(off_uv, gm_w)
    v_raw = proj(off_uv + gm_w, gm_w)
    c_q = proj(off_cq, q_rank)
    c_kv = proj(off_ckv, kv_rank)
    small = proj(off_small, LANES)
    small_ref[...] = small

    u = _gelu(u_raw)
    v = _gelu(v_raw)
    v = _rms(v, gm_gain_ref[...]).astype(BF16)
    n_chunk = tile // GM_CHUNK
    gdim = gm_w // GM_GROUPS
    row = lax.broadcasted_iota(jnp.int32, (GM_CHUNK, GM_CHUNK), 0)
    col = lax.broadcasted_iota(jnp.int32, (GM_CHUNK, GM_CHUNK), 1)
    for g in range(GM_GROUPS):
        w = jnp.where(col <= row, gm_w_ref[g], 0.0).astype(BF16)
        rhs = jnp.concatenate([v[c * GM_CHUNK:(c + 1) * GM_CHUNK, g * gdim:(g + 1) * gdim] for c in range(n_chunk)], axis=1)
        sp = jnp.dot(w, rhs, preferred_element_type=F32)
        bias = gm_b_ref[:, g * gdim:(g + 1) * gdim]
        for c in range(n_chunk):
            rows = slice(c * GM_CHUNK, (c + 1) * GM_CHUNK)
            ya_ref[rows, g * gdim:(g + 1) * gdim] = (
                u[rows, g * gdim:(g + 1) * gdim] * (sp[:, c * gdim:(c + 1) * gdim] + bias)).astype(BF16)

    z_ref[...] = proj(off_z, ssd_inner).astype(BF16)
    lat_t = jnp.concatenate([c_q, c_kv, small], axis=1).T
    qn_t = _rms_rows(lat_t[:q_rank], q_norm_ref[...]).astype(BF16)
    kvn_t = _rms_rows(lat_t[q_rank:q_rank + kv_rank], kv_norm_ref[...]).astype(BF16)
    k_rope_t = lat_t[q_rank + kv_rank + ROPE_LANE0:q_rank + kv_rank + ROPE_LANE0 + MLA_ROPE]
    cos_t = cos_ref[...]
    sin_t = sin_ref[...]
    vt_ref[...] = jnp.dot(w_uv_ref[...], kvn_t, preferred_element_type=F32).astype(BF16)
    q_all = jnp.dot(w_uq_ref[...], qn_t, preferred_element_type=F32)
    k_nope = jnp.dot(w_uk_ref[...], kvn_t, preferred_element_type=F32)
    xbc_ref[...] = proj(off_xbc, conv_dim).astype(BF16)
    q_gain = q_gain_ref[...]
    k_gain = k_gain_ref[...]
    for hd in range(MLA_HEADS):
        qh = q_all[hd * MLA_QK_DIM:(hd + 1) * MLA_QK_DIM]
        qt_ref[hd * LANES:(hd + 1) * LANES, :] = _head_norm_rope(qh, q_gain, cos_t, sin_t).astype(BF16)
        kh = jnp.concatenate([k_nope[hd * MLA_NOPE:(hd + 1) * MLA_NOPE], k_rope_t], axis=0)
        k_ref[:, hd * LANES:(hd + 1) * LANES] = _head_norm_rope(kh, k_gain, cos_t, sin_t).T.astype(BF16)


def _proj(x, cosf, sinf, p, layer, widths):
    n, d = x.shape
    gm_w, q_rank, kv_rank, ssd_inner, conv_dim = widths
    head_w = MLA_HEADS * LANES
    v_w = MLA_HEADS * MLA_V
    tiles = n // ROW_TILE
    res = lambda a: _resident(a.shape, layer)
    table = pl.BlockSpec((MLA_ROPE // 2, ROW_TILE), lambda i: (0, i))
    chan_major = lambda c: pl.BlockSpec((None, c, ROW_TILE), lambda i: (i, 0, 0))
    return pl.pallas_call(
        functools.partial(_proj_kernel, widths=widths),
        grid=(tiles,),
        in_specs=[_rows(d), table, table, res(p["mix_norm"]), res(p["w_proj"]),
                  res(p["gm_gain"]), res(p["gm_w"]), res(p["gm_b"]),
                  res(p["q_norm"]), res(p["kv_norm"]), res(p["w_uq"]), res(p["w_uk"]), res(p["w_uv"]),
                  res(p["q_gain"]), res(p["k_gain"])],
        out_specs=[_rows(gm_w), chan_major(head_w), _rows(head_w), chan_major(v_w),
                   _rows(LANES), _rows(ssd_inner), _rows(conv_dim)],
        out_shape=[jax.ShapeDtypeStruct((n, gm_w), BF16),
                   jax.ShapeDtypeStruct((tiles, head_w, ROW_TILE), BF16),
                   jax.ShapeDtypeStruct((n, head_w), BF16),
                   jax.ShapeDtypeStruct((tiles, v_w, ROW_TILE), BF16),
                   jax.ShapeDtypeStruct((n, LANES), F32),
                   jax.ShapeDtypeStruct((n, ssd_inner), BF16),
                   jax.ShapeDtypeStruct((n, conv_dim), BF16)],
        compiler_params=_params("parallel"),
        name="proj",
    )(x, cosf, sinf, p["mix_norm"], p["w_proj"], p["gm_gain"], p["gm_w"], p["gm_b"],
      p["q_norm"], p["kv_norm"], p["w_uq"], p["w_uk"], p["w_uv"], p["q_gain"], p["k_gain"])


def _attn_kernel(qt_ref, k_ref, vt_ref, o_ref, m_ref, acc_ref):
    n_qt, _, t = qt_ref.shape
    tq = n_qt * t
    qb = pl.program_id(2)

    m_ref[...] = jnp.full(m_ref.shape, -jnp.inf, F32)
    acc_ref[...] = jnp.zeros(acc_ref.shape, F32)
    ones = jnp.ones((ATTN_SUM_ROWS, ATTN_KSUB), BF16)

    def sub_tiles(k_base):
        out = []
        for k0 in range(0, t, ATTN_KSUB):
            for hh in range(2):
                for q0 in range(0, tq, ATTN_QSUB):
                    if k_base is not None and k_base + k0 > q0 + ATTN_QSUB - 1:
                        continue
                    out.append((hh, k0, q0, k_base is not None and k_base + k0 + ATTN_KSUB - 1 > q0))
        return out

    def scores(kj, hh, k0, q0):
        q_sub = qt_ref[q0 // t, hh * LANES:(hh + 1) * LANES, q0 % t:q0 % t + ATTN_QSUB]
        return jnp.dot(k_ref[kj, k0:k0 + ATTN_KSUB, hh * LANES:(hh + 1) * LANES], q_sub, preferred_element_type=F32)

    def absorb(kj, s_next, k_base, after):
        todo = sub_tiles(k_base)
        ahead = list(s_next)
        for i, (hh, k0, q0, masked) in enumerate(todo):
            s = ahead.pop(0)
            if i + ATTN_AHEAD < len(todo):
                ahead.append(scores(kj, *todo[i + ATTN_AHEAD][:3]))
            elif after is not None:
                ahead.append(scores(kj + 1, *after[i + ATTN_AHEAD - len(todo)][:3]))
            qcols = slice(q0, q0 + ATTN_QSUB)
            if masked:
                kpos = k_base + k0 + lax.broadcasted_iota(jnp.int32, s.shape, 0)
                qpos = q0 + lax.broadcasted_iota(jnp.int32, s.shape, 1)
                s = jnp.where(kpos <= qpos, s, -jnp.inf)
            m = m_ref[hh, :, qcols]
            m_new = jnp.maximum(m, jnp.max(s, axis=0, keepdims=True))
            alpha = jnp.exp2(m - m_new)
            p = jnp.exp2((s - m_new).astype(BF16))
            m_ref[hh, :, qcols] = m_new
            v_aug = jnp.concatenate([vt_ref[kj, hh * MLA_V:(hh + 1) * MLA_V, k0:k0 + ATTN_KSUB], ones], axis=0)
            pv = jnp.dot(v_aug, p, preferred_element_type=F32)
            acc_ref[hh, :, qcols] = alpha * acc_ref[hh, :, qcols] + pv
        return tuple(ahead)

    before = sub_tiles(None)
    overlap = [sub_tiles(d * t) for d in range(n_qt)]
    assert [x[:3] for x in overlap[0][:ATTN_AHEAD]] == [x[:3] for x in before[:ATTN_AHEAD]]
    assert all(len(todo) >= ATTN_AHEAD for todo in overlap)
    first = tuple(scores(0, *x[:3]) for x in before[:ATTN_AHEAD])
    n_before = qb * n_qt
    ahead = lax.fori_loop(0, n_before, lambda kj, s: absorb(kj, s, None, before), first)
    for d in range(n_qt):
        ahead = absorb(n_before + d, ahead, d * t, overlap[d + 1] if d + 1 < n_qt else None)
    out = jnp.concatenate([acc_ref[hh, :MLA_V] / acc_ref[hh, MLA_V:MLA_V + 1] for hh in range(2)], axis=0)
    o_ref[...] = out.T.astype(BF16)


def _attention(qt, k, vt, b, s):
    tiles, _, t = qt.shape
    n = k.shape[0]
    pairs = MLA_HEADS // 2
    per_b = s // t
    assert tiles == b * per_b and per_b % ATTN_Q_TILES == 0
    q_blocks = per_b // ATTN_Q_TILES
    tq = ATTN_Q_TILES * t
    return pl.pallas_call(
        _attn_kernel,
        grid=(b, pairs, q_blocks),
        in_specs=[pl.BlockSpec((ATTN_Q_TILES, 2 * LANES, t), lambda bi, j, qb: (bi * q_blocks + qb, j, 0)),
                  pl.BlockSpec((per_b, t, 2 * LANES), lambda bi, j, qb: (bi, 0, j)),
                  pl.BlockSpec((per_b, 2 * MLA_V, t), lambda bi, j, qb: (bi, j, 0))],
        out_specs=pl.BlockSpec((tq, 2 * MLA_V), lambda bi, j, qb: (bi * q_blocks + qb, j)),
        out_shape=jax.ShapeDtypeStruct((n, MLA_HEADS * MLA_V), BF16),
        scratch_shapes=[pltpu.VMEM((2, 1, tq), F32), pltpu.VMEM((2, MLA_V + ATTN_SUM_ROWS, tq), F32)],
        compiler_params=_params("parallel", "parallel", "arbitrary"),
        name="attn",
    )(qt, k.reshape(tiles, t, k.shape[1]), vt)


def _cumsum_lanes(x):
    lane = _lane_iota(x.shape)
    sh = 1
    while sh < x.shape[-1]:
        x = x + jnp.where(lane >= sh, pltpu.roll(x, sh, 1), 0.0)
        sh *= 2
    return x


def _ssd_kernel(xbc_ref, z_ref, small_ref, conv_w_ref, conv_b_ref, dt_bias_ref, a_log_ref, d_skip_ref, gain_ref,
                y_ref, ext_ref, state_ref):
    tile = xbc_ref.shape[0]
    inner = z_ref.shape[1]
    gstate = SSD_GROUPS * SSD_STATE

    @pl.when(pl.program_id(1) == 0)
    def _():
        ext_ref[0:SSD_CHUNK, :] = jnp.zeros((SSD_CHUNK, ext_ref.shape[1]), BF16)
        state_ref[...] = jnp.zeros(state_ref.shape, F32)

    ext_ref[SSD_CHUNK:SSD_CHUNK + tile, :] = xbc_ref[...]

    n_shift = SSD_CONV - 1
    srow = lax.broadcasted_iota(jnp.int32, (SSD_CHUNK, 2 * SSD_CHUNK), 0)
    scol = lax.broadcasted_iota(jnp.int32, (SSD_CHUNK, 2 * SSD_CHUNK), 1)
    pick = jnp.concatenate([jnp.where(scol == SSD_CHUNK + srow - (d + 1), 1.0, 0.0).astype(BF16)
                            for d in range(n_shift)], axis=0)

    def conv_chunk(c):
        window = ext_ref[c * SSD_CHUNK:(c + 2) * SSD_CHUNK, :]
        shifted = jnp.dot(pick, window, preferred_element_type=F32)
        conv = conv_b_ref[...] + conv_w_ref[n_shift:n_shift + 1, :] * window[SSD_CHUNK:].astype(F32)
        for d in range(n_shift):
            conv = conv + conv_w_ref[n_shift - 1 - d:n_shift - d, :] * shifted[d * SSD_CHUNK:(d + 1) * SSD_CHUNK]
        return conv * jax.nn.sigmoid(conv)

    lane = _lane_iota((tile, LANES))
    head_lane = (lane >= DT_LANE0) & (lane < DT_LANE0 + SSD_HEADS)
    dt = jnp.where(head_lane, jax.nn.softplus(small_ref[...] + dt_bias_ref[...]), 0.0)
    da = dt * (-jnp.exp(a_log_ref[...]))

    sub = lax.broadcasted_iota(jnp.int32, (SSD_CHUNK, SSD_CHUNK), 0)
    lan = lax.broadcasted_iota(jnp.int32, (SSD_CHUNK, SSD_CHUNK), 1)
    causal = lan <= sub
    low_half = lan < SSD_HEAD_DIM
    top_half = sub < SSD_HEAD_DIM
    heads_per_group = SSD_HEADS // SSD_GROUPS

    for c in range(tile // SSD_CHUNK):
        rows = slice(c * SSD_CHUNK, (c + 1) * SSD_CHUNK)
        cs_row = _cumsum_lanes(da[rows].T[DT_LANE0:DT_LANE0 + SSD_HEADS])
        cs_col = jnp.concatenate([cs_row, jnp.zeros((SSD_CHUNK - SSD_HEADS, SSD_CHUNK), F32)], axis=0).T
        dt_c = dt[rows]
        xc = conv_chunk(c)
        b_in = [xc[:, inner + g * SSD_STATE:inner + (g + 1) * SSD_STATE].astype(BF16) for g in range(SSD_GROUPS)]
        c_in = [xc[:, inner + gstate + g * SSD_STATE:inner + gstate + (g + 1) * SSD_STATE].astype(BF16)
                for g in range(SSD_GROUPS)]
        cb = [lax.dot_general(c_in[g], b_in[g], (((1,), (1,)), ((), ())), preferred_element_type=F32)
              for g in range(SSD_GROUPS)]
        for pr in range(SSD_HEADS // 2):
            g = (2 * pr) // heads_per_group
            lanes = slice(pr * LANES, (pr + 1) * LANES)
            xs = xc[:, lanes]
            col = [cs_col[:, 2 * pr + i:2 * pr + i + 1] for i in range(2)]
            row = [cs_row[2 * pr + i:2 * pr + i + 1, :] for i in range(2)]
            dtc = [dt_c[:, 2 * pr + i:2 * pr + i + 1] for i in range(2)]
            tot = [cs_col[SSD_CHUNK - 1:SSD_CHUNK, 2 * pr + i:2 * pr + i + 1] for i in range(2)]
            xdt = xs * jnp.where(low_half, dtc[0], dtc[1])
            xdt_b = xdt.astype(BF16)
            y_d = []
            for i in range(2):
                lmat = jnp.exp(jnp.where(causal, col[i] - row[i], -jnp.inf))
                y_d.append(jnp.dot((cb[g] * lmat).astype(BF16), xdt_b, preferred_element_type=F32))
            y_diag = jnp.where(low_half, y_d[0], y_d[1])
            decay = jnp.where(low_half, jnp.exp(tot[0] - col[0]), jnp.exp(tot[1] - col[1]))
            states = lax.dot_general((xdt * decay).astype(BF16), b_in[g], (((0,), (0,)), ((), ())),
                                     preferred_element_type=F32)
            s_in = state_ref[pr]
            y_off = lax.dot_general(c_in[g], s_in.astype(BF16), (((1,), (1,)), ((), ())), preferred_element_type=F32)
            y_off = y_off * jnp.where(low_half, jnp.exp(col[0]), jnp.exp(col[1]))
            state_ref[pr] = s_in * jnp.where(top_half, jnp.exp(tot[0]), jnp.exp(tot[1])) + states
            y = y_diag + y_off + xs * d_skip_ref[:, lanes]
            zz = z_ref[rows, lanes].astype(F32)
            y_ref[rows, lanes] = y * (zz * jax.nn.sigmoid(zz))

    ext_ref[0:SSD_CHUNK, :] = ext_ref[tile:tile + SSD_CHUNK, :]

    gw = inner // SSD_GROUPS
    for g in range(SSD_GROUPS):
        lanes = slice(g * gw, (g + 1) * gw)
        y_ref[:, lanes] = _rms(y_ref[:, lanes], gain_ref[:, lanes])


def _ssd(xbc, z, small, p, layer, b, s):
    n, conv_dim = xbc.shape
    inner = z.shape[1]
    tile = min(SSD_TILE, s)
    assert s % tile == 0 and tile % SSD_CHUNK == 0
    per_b = s // tile
    rows = lambda w: pl.BlockSpec((tile, w), lambda bi, si: (bi * per_b + si, 0))
    res = lambda a: _resident(a.shape, layer)
    return pl.pallas_call(
        _ssd_kernel,
        grid=(b, per_b),
        in_specs=[rows(conv_dim), rows(inner), rows(LANES), res(p["conv_w"]), res(p["conv_b"]), res(p["dt_bias"]),
                  res(p["a_log"]), res(p["d_skip"]), res(p["ssd_gain"])],
        out_specs=rows(inner),
        out_shape=jax.ShapeDtypeStruct((n, inner), F32),
        scratch_shapes=[pltpu.VMEM((SSD_CHUNK + tile, conv_dim), BF16),
                        pltpu.VMEM((SSD_HEADS // 2, 2 * SSD_HEAD_DIM, SSD_STATE), F32)],
        compiler_params=_params("parallel", "arbitrary"),
        name="ssd",
    )(xbc, z, small, p["conv_w"], p["conv_b"], p["dt_bias"], p["a_log"], p["d_skip"], p["ssd_gain"])


def _merge_kernel(x_ref, ya_ref, yb_ref, yc_ref, mix_gain_ref, w_gate_ref, w_branch_ref, w_out_ref, o_ref):
    x = x_ref[...]
    d = x.shape[1]
    h = _rms(x, mix_gain_ref[...]).astype(BF16)
    merged = None
    for i, y_ref in enumerate((ya_ref, yb_ref, yc_ref)):
        gate = jax.nn.sigmoid(jnp.dot(h, w_gate_ref[:, i * d:(i + 1) * d], preferred_element_type=F32))
        term = gate * jnp.dot(y_ref[...].astype(BF16), w_branch_ref[i], preferred_element_type=F32)
        merged = term if merged is None else merged + term
    o_ref[...] = x + jnp.dot(merged.astype(BF16), w_out_ref[...], preferred_element_type=F32)


def _merge(x, ya, yb, yc, p, layer):
    n, d = x.shape
    res = lambda a: _resident(a.shape, layer)
    return pl.pallas_call(
        _merge_kernel,
        grid=(n // ROW_TILE,),
        in_specs=[_rows(d), _rows(ya.shape[1]), _rows(yb.shape[1]), _rows(yc.shape[1]),
                  res(p["mix_norm"]), res(p["w_gate"]), res(p["w_branch"]), res(p["w_out"])],
        out_specs=_rows(d),
        out_shape=jax.ShapeDtypeStruct((n, d), F32),
        compiler_params=_params("parallel"),
        name="merge",
    )(x, ya, yb, yc, p["mix_norm"], p["w_gate"], p["w_branch"], p["w_out"])


def _row(a):
    return a[:, None, :]


def _col(a):
    return a[:, :, None]


def _pad_lanes(a, lane0):
    return _row(jnp.pad(a, ((0, 0), (lane0, LANES - lane0 - a.shape[1]))))


def kernel(x, positions, ffn1_norm, ffn1_w_in, ffn1_w_out, mix_norm, w_in, gm_v_norm, gm_w_s, gm_b_s, mla_q_norm, mla_kv_norm, mla_w_uq, mla_w_ukv, mla_q_gain, mla_k_gain, ssd_conv_w, ssd_conv_b, ssd_dt_bias, ssd_a_log, ssd_d, ssd_norm, w_branch, w_out, ffn2_norm, ffn2_w_in, ffn2_w_out):
    b, s, d = x.shape
    depth = w_in.shape[0]
    n = b * s
    gm_w = gm_v_norm.shape[1]
    q_rank = mla_q_norm.shape[1]
    kv_rank = mla_kv_norm.shape[1]
    ssd_inner = ssd_norm.shape[1]
    conv_dim = ssd_conv_b.shape[1]
    widths = (gm_w, q_rank, kv_rank, ssd_inner, conv_dim)

    sizes = (2 * gm_w, q_rank, kv_rank, MLA_ROPE, ssd_inner, conv_dim, SSD_HEADS, N_BRANCH * d)
    offs = [0]
    for w in sizes:
        offs.append(offs[-1] + w)
    col = lambda i: w_in[:, :, offs[i]:offs[i + 1]]
    w_small = jnp.zeros((depth, d, LANES), w_in.dtype)
    w_small = w_small.at[:, :, DT_LANE0:DT_LANE0 + SSD_HEADS].set(col(6))
    w_small = w_small.at[:, :, ROPE_LANE0:ROPE_LANE0 + MLA_ROPE].set(col(3))
    w_proj = jnp.concatenate([col(0), col(1), col(2), w_small, col(4), col(5)], axis=-1).astype(BF16)

    w_ukv = mla_w_ukv.reshape(depth, kv_rank, MLA_HEADS, MLA_NOPE + MLA_V)
    out_major = lambda w: w.reshape(depth, kv_rank, -1).transpose(0, 2, 1).astype(BF16)
    p = {
        "mix_norm": _row(mix_norm),
        "w_proj": w_proj,
        "w_gate": col(7).astype(BF16),
        "gm_gain": _row(gm_v_norm),
        "gm_w": gm_w_s,
        "gm_b": jnp.repeat(gm_b_s.transpose(0, 2, 1), gm_w // GM_GROUPS, axis=2),
        "q_norm": _col(mla_q_norm),
        "kv_norm": _col(mla_kv_norm),
        "w_uq": mla_w_uq.transpose(0, 2, 1).astype(BF16),
        "w_uk": out_major(w_ukv[..., :MLA_NOPE]),
        "w_uv": out_major(w_ukv[..., MLA_NOPE:]),
        "q_gain": _col(mla_q_gain * (MLA_QK_DIM ** -0.5 * LOG2_E)),
        "k_gain": _col(mla_k_gain),
        "conv_w": ssd_conv_w,
        "conv_b": _row(ssd_conv_b),
        "dt_bias": _pad_lanes(ssd_dt_bias, DT_LANE0),
        "a_log": _pad_lanes(ssd_a_log, DT_LANE0),
        "d_skip": _row(jnp.repeat(ssd_d, SSD_HEAD_DIM, axis=1)),
        "ssd_gain": _row(ssd_norm),
        "w_branch": w_branch.astype(BF16),
        "w_out": w_out.astype(BF16),
    }
    ffn = [(_row(ffn1_norm), ffn1_w_in.astype(BF16), ffn1_w_out.astype(BF16)),
           (_row(ffn2_norm), ffn2_w_in.astype(BF16), ffn2_w_out.astype(BF16))]

    cosf, sinf = _rope_tables(positions)
    xf = x.reshape(n, d)
    for layer in range(depth):
        xf = _ffn(xf, *ffn[0], layer)
        ya, qt, k, vt, small, z, xbc = _proj(xf, cosf, sinf, p, layer, widths)
        yb = _attention(qt, k, vt, b, s)
        yc = _ssd(xbc, z, small, p, layer, b, s)
        xf = _merge(xf, ya, yb, yc, p, layer)
        xf = _ffn(xf, *ffn[1], layer)
    return xf.reshape(b, s, d)
```

```python
import functools

import jax
import jax.numpy as jnp
from jax import lax
from jax.experimental import pallas as pl
from jax.experimental.pallas import tpu as pltpu

F32 = jnp.float32
BF16 = jnp.bfloat16

LANES = 128
VMEM_LIMIT_BYTES = 56 * 1024 * 1024

EPS = 1e-6
FFN_RESID = 0.5
GM_GROUPS = 4
GM_CHUNK = 128
MLA_HEADS = 8
MLA_NOPE = 64
MLA_ROPE = 32
MLA_QK_DIM = MLA_NOPE + MLA_ROPE
MLA_V = 64
ROPE_THETA = 10000.0
LOG2_E = 1.4426950408889634
SSD_HEADS = 8
SSD_HEAD_DIM = 64
SSD_GROUPS = 2
SSD_STATE = 128
SSD_CONV = 4
SSD_CHUNK = 128
N_BRANCH = 3
DT_LANE0 = 0
ROPE_LANE0 = MLA_NOPE

ROW_TILE = 512
ATTN_KSUB = 128
ATTN_QSUB = 256
ATTN_AHEAD = 6
ATTN_SUM_ROWS = 16
ATTN_Q_TILES = 8
SSD_TILE = 512
FF_CHUNK = 256


def _rms(x, gain):
    ms = jnp.mean(x * x, axis=-1, keepdims=True)
    return x * lax.rsqrt(ms + EPS) * gain


def _gelu(x):
    return 0.5 * x * (1.0 + lax.erf(x * (2.0 ** -0.5)))


def _resident(shape, layer):
    zeros = (0,) * (len(shape) - 1)
    return pl.BlockSpec((None,) + tuple(shape[1:]), lambda *_: (layer,) + zeros, pipeline_mode=pl.Buffered(1))


def _rows(width, tile=ROW_TILE):
    return pl.BlockSpec((tile, width), lambda i: (i, 0))


def _params(*sem):
    return pltpu.CompilerParams(dimension_semantics=sem, vmem_limit_bytes=VMEM_LIMIT_BYTES)


def _rope_table_kernel(pos_ref, freq_ref, cos_ref, sin_ref):
    ang = freq_ref[...] * pos_ref[...].astype(F32)
    cos_ref[...] = jnp.cos(ang)
    sin_ref[...] = jnp.sin(ang)


def _rope_tables(positions):
    n = positions.size
    half = MLA_ROPE // 2
    inv_freq = 1.0 / (ROPE_THETA ** (jnp.arange(0, MLA_ROPE, 2, dtype=F32) / MLA_ROPE))
    tile = min(4096, n)
    return pl.pallas_call(
        _rope_table_kernel,
        grid=(n // tile,),
        in_specs=[pl.BlockSpec((1, tile), lambda i: (0, i)),
                  pl.BlockSpec((half, 1), lambda i: (0, 0))],
        out_specs=[pl.BlockSpec((half, tile), lambda i: (0, i))] * 2,
        out_shape=[jax.ShapeDtypeStruct((half, n), F32)] * 2,
        compiler_params=_params("parallel"),
        name="rope_tables",
    )(positions.reshape(1, n), inv_freq.reshape(half, 1))


def _ffn_body(x, gain, w_in_ref, w_out_ref, hid_ref):
    d_ff = w_out_ref.shape[0]
    h = _rms(x, gain).astype(BF16)
    for c in range(d_ff // FF_CHUNK):
        lo = c * FF_CHUNK
        g = jnp.dot(h, w_in_ref[:, lo:lo + FF_CHUNK], preferred_element_type=F32)
        u = jnp.dot(h, w_in_ref[:, d_ff + lo:d_ff + lo + FF_CHUNK], preferred_element_type=F32)
        hid_ref[:, lo:lo + FF_CHUNK] = (g * jax.nn.sigmoid(g) * u).astype(BF16)
    y = jnp.dot(hid_ref[...], w_out_ref[...], preferred_element_type=F32)
    return x + FFN_RESID * y


def _ffn_kernel(x_ref, gain_ref, w_in_ref, w_out_ref, o_ref, hid_ref):
    o_ref[...] = _ffn_body(x_ref[...], gain_ref[...], w_in_ref, w_out_ref, hid_ref)


def _ffn(x, gain, w_in, w_out, layer):
    n, d = x.shape
    d_ff = w_out.shape[1]
    assert d_ff % FF_CHUNK == 0 and n % ROW_TILE == 0
    return pl.pallas_call(
        _ffn_kernel,
        grid=(n // ROW_TILE,),
        in_specs=[_rows(d), _resident(gain.shape, layer), _resident(w_in.shape, layer), _resident(w_out.shape, layer)],
        out_specs=_rows(d),
        out_shape=jax.ShapeDtypeStruct((n, d), F32),
        scratch_shapes=[pltpu.VMEM((ROW_TILE, d_ff), BF16)],
        compiler_params=_params("parallel"),
        name="ffn",
    )(x, gain, w_in, w_out)


def _lane_iota(shape):
    return lax.broadcasted_iota(jnp.int32, shape, len(shape) - 1)


def _head_norm_rope(x, gain, cos_t, sin_t):
    half = MLA_ROPE // 2
    ms = jnp.sum(x * x, axis=0, keepdims=True) * (1.0 / MLA_QK_DIM)
    y = x * lax.rsqrt(ms + EPS) * gain
    x1 = y[MLA_NOPE:MLA_NOPE + half]
    x2 = y[MLA_NOPE + half:MLA_QK_DIM]
    pad = jnp.zeros((LANES - MLA_QK_DIM, x.shape[1]), F32)
    return jnp.concatenate([y[:MLA_NOPE], x1 * cos_t - x2 * sin_t, x2 * cos_t + x1 * sin_t, pad], axis=0)


def _rms_rows(x, gain):
    ms = jnp.mean(x * x, axis=0, keepdims=True)
    return x * lax.rsqrt(ms + EPS) * gain


def _proj_kernel(x_ref, cos_ref, sin_ref, mix_gain_ref, w_proj_ref,
                 gm_gain_ref, gm_w_ref, gm_b_ref,
                 q_norm_ref, kv_norm_ref, w_uq_ref, w_uk_ref, w_uv_ref, q_gain_ref, k_gain_ref,
                 ya_ref, qt_ref, k_ref, vt_ref, small_ref, z_ref, xbc_ref,
                 *, widths):
    gm_w, q_rank, kv_rank, ssd_inner, conv_dim = widths
    tile = x_ref.shape[0]
    h = _rms(x_ref[...], mix_gain_ref[...]).astype(BF16)

    def proj(lo, width):
        return jnp.dot(h, w_proj_ref[:, lo:lo + width], preferred_element_type=F32)

    off_uv = 0
    off_cq = off_uv + 2 * gm_w
    off_ckv = off_cq + q_rank
    off_small = off_ckv + kv_rank
    off_z = off_small + LANES
    off_xbc = off_z + ssd_inner

    u_raw = proj(off_uv, gm_w)
    v_raw = proj(off_uv + gm_w, gm_w)
    c_q = proj(off_cq, q_rank)
    c_kv = proj(off_ckv, kv_rank)
    small = proj(off_small, LANES)
    small_ref[...] = small

    u = _gelu(u_raw)
    v = _gelu(v_raw)
    v = _rms(v, gm_gain_ref[...]).astype(BF16)
    n_chunk = tile // GM_CHUNK
    gdim = gm_w // GM_GROUPS
    row = lax.broadcasted_iota(jnp.int32, (GM_CHUNK, GM_CHUNK), 0)
    col = lax.broadcasted_iota(jnp.int32, (GM_CHUNK, GM_CHUNK), 1)
    for g in range(GM_GROUPS):
        w = jnp.where(col <= row, gm_w_ref[g], 0.0).astype(BF16)
        rhs = jnp.concatenate([v[c * GM_CHUNK:(c + 1) * GM_CHUNK, g * gdim:(g + 1) * gdim] for c in range(n_chunk)], axis=1)
        sp = jnp.dot(w, rhs, preferred_element_type=F32)
        bias = gm_b_ref[:, g * gdim:(g + 1) * gdim]
        for c in range(n_chunk):
            rows = slice(c * GM_CHUNK, (c + 1) * GM_CHUNK)
            ya_ref[rows, g * gdim:(g + 1) * gdim] = (
                u[rows, g * gdim:(g + 1) * gdim] * (sp[:, c * gdim:(c + 1) * gdim] + bias)).astype(BF16)

    z_ref[...] = proj(off_z, ssd_inner).astype(BF16)
    lat_t = jnp.concatenate([c_q, c_kv, small], axis=1).T
    qn_t = _rms_rows(lat_t[:q_rank], q_norm_ref[...]).astype(BF16)
    kvn_t = _rms_rows(lat_t[q_rank:q_rank + kv_rank], kv_norm_ref[...]).astype(BF16)
    k_rope_t = lat_t[q_rank + kv_rank + ROPE_LANE0:q_rank + kv_rank + ROPE_LANE0 + MLA_ROPE]
    cos_t = cos_ref[...]
    sin_t = sin_ref[...]
    vt_ref[...] = jnp.dot(w_uv_ref[...], kvn_t, preferred_element_type=F32).astype(BF16)
    q_all = jnp.dot(w_uq_ref[...], qn_t, preferred_element_type=F32)
    k_nope = jnp.dot(w_uk_ref[...], kvn_t, preferred_element_type=F32)
    xbc_ref[...] = proj(off_xbc, conv_dim).astype(BF16)
    q_gain = q_gain_ref[...]
    k_gain = k_gain_ref[...]
    for hd in range(MLA_HEADS):
        qh = q_all[hd * MLA_QK_DIM:(hd + 1) * MLA_QK_DIM]
        qt_ref[hd * LANES:(hd + 1) * LANES, :] = _head_norm_rope(qh, q_gain, cos_t, sin_t).astype(BF16)
        kh = jnp.concatenate([k_nope[hd * MLA_NOPE:(hd + 1) * MLA_NOPE], k_rope_t], axis=0)
        k_ref[:, hd * LANES:(hd + 1) * LANES] = _head_norm_rope(kh, k_gain, cos_t, sin_t).T.astype(BF16)


def _proj(x, cosf, sinf, p, layer, widths):
    n, d = x.shape
    gm_w, q_rank, kv_rank, ssd_inner, conv_dim = widths
    head_w = MLA_HEADS * LANES
    v_w = MLA_HEADS * MLA_V
    tiles = n // ROW_TILE
    res = lambda a: _resident(a.shape, layer)
    table = pl.BlockSpec((MLA_ROPE // 2, ROW_TILE), lambda i: (0, i))
    chan_major = lambda c: pl.BlockSpec((None, c, ROW_TILE), lambda i: (i, 0, 0))
    return pl.pallas_call(
        functools.partial(_proj_kernel, widths=widths),
        grid=(tiles,),
        in_specs=[_rows(d), table, table, res(p["mix_norm"]), res(p["w_proj"]),
                  res(p["gm_gain"]), res(p["gm_w"]), res(p["gm_b"]),
                  res(p["q_norm"]), res(p["kv_norm"]), res(p["w_uq"]), res(p["w_uk"]), res(p["w_uv"]),
                  res(p["q_gain"]), res(p["k_gain"])],
        out_specs=[_rows(gm_w), chan_major(head_w), _rows(head_w), chan_major(v_w),
                   _rows(LANES), _rows(ssd_inner), _rows(conv_dim)],
        out_shape=[jax.ShapeDtypeStruct((n, gm_w), BF16),
                   jax.ShapeDtypeStruct((tiles, head_w, ROW_TILE), BF16),
                   jax.ShapeDtypeStruct((n, head_w), BF16),
                   jax.ShapeDtypeStruct((tiles, v_w, ROW_TILE), BF16),
                   jax.ShapeDtypeStruct((n, LANES), F32),
                   jax.ShapeDtypeStruct((n, ssd_inner), BF16),
                   jax.ShapeDtypeStruct((n, conv_dim), BF16)],
        compiler_params=_params("parallel"),
        name="proj",
    )(x, cosf, sinf, p["mix_norm"], p["w_proj"], p["gm_gain"], p["gm_w"], p["gm_b"],
      p["q_norm"], p["kv_norm"], p["w_uq"], p["w_uk"], p["w_uv"], p["q_gain"], p["k_gain"])


def _attn_kernel(qt_ref, k_ref, vt_ref, o_ref, m_ref, acc_ref):
    n_qt, _, t = qt_ref.shape
    tq = n_qt * t
    qb = pl.program_id(2)

    m_ref[...] = jnp.full(m_ref.shape, -jnp.inf, F32)
    acc_ref[...] = jnp.zeros(acc_ref.shape, F32)
    ones = jnp.ones((ATTN_SUM_ROWS, ATTN_KSUB), BF16)

    def sub_tiles(k_base):
        out = []
        for k0 in range(0, t, ATTN_KSUB):
            for hh in range(2):
                for q0 in range(0, tq, ATTN_QSUB):
                    if k_base is not None and k_base + k0 > q0 + ATTN_QSUB - 1:
                        continue
                    out.append((hh, k0, q0, k_base is not None and k_base + k0 + ATTN_KSUB - 1 > q0))
        return out

    def scores(kj, hh, k0, q0):
        q_sub = qt_ref[q0 // t, hh * LANES:(hh + 1) * LANES, q0 % t:q0 % t + ATTN_QSUB]
        return jnp.dot(k_ref[kj, k0:k0 + ATTN_KSUB, hh * LANES:(hh + 1) * LANES], q_sub, preferred_element_type=F32)

    def absorb(kj, s_next, k_base, after):
        todo = sub_tiles(k_base)
        ahead = list(s_next)
        for i, (hh, k0, q0, masked) in enumerate(todo):
            s = ahead.pop(0)
            if i + ATTN_AHEAD < len(todo):
                ahead.append(scores(kj, *todo[i + ATTN_AHEAD][:3]))
            elif after is not None:
                ahead.append(scores(kj + 1, *after[i + ATTN_AHEAD - len(todo)][:3]))
            qcols = slice(q0, q0 + ATTN_QSUB)
            if masked:
                kpos = k_base + k0 + lax.broadcasted_iota(jnp.int32, s.shape, 0)
                qpos = q0 + lax.broadcasted_iota(jnp.int32, s.shape, 1)
                s = jnp.where(kpos <= qpos, s, -jnp.inf)
            m = m_ref[hh, :, qcols]
            m_new = jnp.maximum(m, jnp.max(s, axis=0, keepdims=True))
            alpha = jnp.exp2(m - m_new)
            p = jnp.exp2((s - m_new).astype(BF16))
            m_ref[hh, :, qcols] = m_new
            v_aug = jnp.concatenate([vt_ref[kj, hh * MLA_V:(hh + 1) * MLA_V, k0:k0 + ATTN_KSUB], ones], axis=0)
            pv = jnp.dot(v_aug, p, preferred_element_type=F32)
            acc_ref[hh, :, qcols] = alpha * acc_ref[hh, :, qcols] + pv
        return tuple(ahead)

    before = sub_tiles(None)
    overlap = [sub_tiles(d * t) for d in range(n_qt)]
    assert [x[:3] for x in overlap[0][:ATTN_AHEAD]] == [x[:3] for x in before[:ATTN_AHEAD]]
    assert all(len(todo) >= ATTN_AHEAD for todo in overlap)
    first = tuple(scores(0, *x[:3]) for x in before[:ATTN_AHEAD])
    n_before = qb * n_qt
    ahead = lax.fori_loop(0, n_before, lambda kj, s: absorb(kj, s, None, before), first)
    for d in range(n_qt):
        ahead = absorb(n_before + d, ahead, d * t, overlap[d + 1] if d + 1 < n_qt else None)
    out = jnp.concatenate([acc_ref[hh, :MLA_V] / acc_ref[hh, MLA_V:MLA_V + 1] for hh in range(2)], axis=0)
    o_ref[...] = out.T.astype(BF16)


def _attention(qt, k, vt, b, s):
    tiles, _, t = qt.shape
    n = k.shape[0]
    pairs = MLA_HEADS // 2
    per_b = s // t
    assert tiles == b * per_b and per_b % ATTN_Q_TILES == 0
    q_blocks = per_b // ATTN_Q_TILES
    tq = ATTN_Q_TILES * t
    return pl.pallas_call(
        _attn_kernel,
        grid=(b, pairs, q_blocks),
        in_specs=[pl.BlockSpec((ATTN_Q_TILES, 2 * LANES, t), lambda bi, j, qb: (bi * q_blocks + qb, j, 0)),
                  pl.BlockSpec((per_b, t, 2 * LANES), lambda bi, j, qb: (bi, 0, j)),
                  pl.BlockSpec((per_b, 2 * MLA_V, t), lambda bi, j, qb: (bi, j, 0))],
        out_specs=pl.BlockSpec((tq, 2 * MLA_V), lambda bi, j, qb: (bi * q_blocks + qb, j)),
        out_shape=jax.ShapeDtypeStruct((n, MLA_HEADS * MLA_V), BF16),
        scratch_shapes=[pltpu.VMEM((2, 1, tq), F32), pltpu.VMEM((2, MLA_V + ATTN_SUM_ROWS, tq), F32)],
        compiler_params=_params("parallel", "parallel", "arbitrary"),
        name="attn",
    )(qt, k.reshape(tiles, t, k.shape[1]), vt)


def _cumsum_lanes(x):
    lane = _lane_iota(x.shape)
    sh = 1
    while sh < x.shape[-1]:
        x = x + jnp.where(lane >= sh, pltpu.roll(x, sh, 1), 0.0)
        sh *= 2
    return x


def _ssd_kernel(xbc_ref, z_ref, small_ref, conv_w_ref, conv_b_ref, dt_bias_ref, a_log_ref, d_skip_ref, gain_ref,
                y_ref, ext_ref, state_ref):
    tile = xbc_ref.shape[0]
    inner = z_ref.shape[1]
    gstate = SSD_GROUPS * SSD_STATE

    @pl.when(pl.program_id(1) == 0)
    def _():
        ext_ref[0:SSD_CHUNK, :] = jnp.zeros((SSD_CHUNK, ext_ref.shape[1]), BF16)
        state_ref[...] = jnp.zeros(state_ref.shape, F32)

    ext_ref[SSD_CHUNK:SSD_CHUNK + tile, :] = xbc_ref[...]

    n_shift = SSD_CONV - 1
    srow = lax.broadcasted_iota(jnp.int32, (SSD_CHUNK, 2 * SSD_CHUNK), 0)
    scol = lax.broadcasted_iota(jnp.int32, (SSD_CHUNK, 2 * SSD_CHUNK), 1)
    pick = jnp.concatenate([jnp.where(scol == SSD_CHUNK + srow - (d + 1), 1.0, 0.0).astype(BF16)
                            for d in range(n_shift)], axis=0)

    def conv_chunk(c):
        window = ext_ref[c * SSD_CHUNK:(c + 2) * SSD_CHUNK, :]
        shifted = jnp.dot(pick, window, preferred_element_type=F32)
        conv = conv_b_ref[...] + conv_w_ref[n_shift:n_shift + 1, :] * window[SSD_CHUNK:].astype(F32)
        for d in range(n_shift):
            conv = conv + conv_w_ref[n_shift - 1 - d:n_shift - d, :] * shifted[d * SSD_CHUNK:(d + 1) * SSD_CHUNK]
        return conv * jax.nn.sigmoid(conv)

    lane = _lane_iota((tile, LANES))
    head_lane = (lane >= DT_LANE0) & (lane < DT_LANE0 + SSD_HEADS)
    dt = jnp.where(head_lane, jax.nn.softplus(small_ref[...] + dt_bias_ref[...]), 0.0)
    da = dt * (-jnp.exp(a_log_ref[...]))

    sub = lax.broadcasted_iota(jnp.int32, (SSD_CHUNK, SSD_CHUNK), 0)
    lan = lax.broadcasted_iota(jnp.int32, (SSD_CHUNK, SSD_CHUNK), 1)
    causal = lan <= sub
    low_half = lan < SSD_HEAD_DIM
    heads_per_group = SSD_HEADS // SSD_GROUPS

    xc_next = conv_chunk(0)
    for c in range(tile // SSD_CHUNK):
        rows = slice(c * SSD_CHUNK, (c + 1) * SSD_CHUNK)
        cs_row = _cumsum_lanes(da[rows].T[DT_LANE0:DT_LANE0 + SSD_HEADS])
        cs_col = jnp.concatenate([cs_row, jnp.zeros((SSD_CHUNK - SSD_HEADS, SSD_CHUNK), F32)], axis=0).T
        dt_c = dt[rows]
        xc = xc_next
        if c + 1 < tile // SSD_CHUNK:
            xc_next = conv_chunk(c + 1)
        b_t = [xc[:, inner + g * SSD_STATE:inner + (g + 1) * SSD_STATE].T.astype(BF16) for g in range(SSD_GROUPS)]
        c_in = [xc[:, inner + gstate + g * SSD_STATE:inner + gstate + (g + 1) * SSD_STATE].astype(BF16)
                for g in range(SSD_GROUPS)]
        cb = [jnp.dot(c_in[g], b_t[g], preferred_element_type=F32) for g in range(SSD_GROUPS)]
        for pr in range(SSD_HEADS // 2):
            g = (2 * pr) // heads_per_group
            lanes = slice(pr * LANES, (pr + 1) * LANES)
            xs = xc[:, lanes]
            col = [cs_col[:, 2 * pr + i:2 * pr + i + 1] for i in range(2)]
            row = [cs_row[2 * pr + i:2 * pr + i + 1, :] for i in range(2)]
            dtc = [dt_c[:, 2 * pr + i:2 * pr + i + 1] for i in range(2)]
            tot = [cs_col[SSD_CHUNK - 1:SSD_CHUNK, 2 * pr + i:2 * pr + i + 1] for i in range(2)]
            xdt = xs * jnp.where(low_half, dtc[0], dtc[1])
            xdt_b = xdt.astype(BF16)
            y_d = []
            for i in range(2):
                lmat = jnp.exp(jnp.where(causal, col[i] - row[i], -jnp.inf))
                y_d.append(jnp.dot((cb[g] * lmat).astype(BF16), xdt_b, preferred_element_type=F32))
            y_diag = jnp.where(low_half, y_d[0], y_d[1])
            decay = jnp.where(low_half, jnp.exp(tot[0] - col[0]), jnp.exp(tot[1] - col[1]))
            states = jnp.dot(b_t[g], (xdt * decay).astype(BF16), preferred_element_type=F32)
            s_in = state_ref[pr]
            y_off = jnp.dot(c_in[g], s_in.astype(BF16), preferred_element_type=F32)
            y_off = y_off * jnp.where(low_half, jnp.exp(col[0]), jnp.exp(col[1]))
            state_ref[pr] = s_in * jnp.where(low_half, jnp.exp(tot[0]), jnp.exp(tot[1])) + states
            y = y_diag + y_off + xs * d_skip_ref[:, lanes]
            zz = z_ref[rows, lanes].astype(F32)
            y_ref[rows, lanes] = y * (zz * jax.nn.sigmoid(zz))

    ext_ref[0:SSD_CHUNK, :] = ext_ref[tile:tile + SSD_CHUNK, :]

    gw = inner // SSD_GROUPS
    for g in range(SSD_GROUPS):
        lanes = slice(g * gw, (g + 1) * gw)
        y_ref[:, lanes] = _rms(y_ref[:, lanes], gain_ref[:, lanes])


def _ssd(xbc, z, small, p, layer, b, s):
    n, conv_dim = xbc.shape
    inner = z.shape[1]
    tile = min(SSD_TILE, s)
    assert s % tile == 0 and tile % SSD_CHUNK == 0
    per_b = s // tile
    rows = lambda w: pl.BlockSpec((tile, w), lambda bi, si: (bi * per_b + si, 0))
    res = lambda a: _resident(a.shape, layer)
    return pl.pallas_call(
        _ssd_kernel,
        grid=(b, per_b),
        in_specs=[rows(conv_dim), rows(inner), rows(LANES), res(p["conv_w"]), res(p["conv_b"]), res(p["dt_bias"]),
                  res(p["a_log"]), res(p["d_skip"]), res(p["ssd_gain"])],
        out_specs=rows(inner),
        out_shape=jax.ShapeDtypeStruct((n, inner), F32),
        scratch_shapes=[pltpu.VMEM((SSD_CHUNK + tile, conv_dim), BF16),
                        pltpu.VMEM((SSD_HEADS // 2, SSD_STATE, 2 * SSD_HEAD_DIM), F32)],
        compiler_params=_params("parallel", "arbitrary"),
        name="ssd",
    )(xbc, z, small, p["conv_w"], p["conv_b"], p["dt_bias"], p["a_log"], p["d_skip"], p["ssd_gain"])


def _merge_kernel(x_ref, ya_ref, yb_ref, yc_ref, mix_gain_ref, w_gate_ref, w_branch_ref, w_out_ref, o_ref):
    x = x_ref[...]
    d = x.shape[1]
    h = _rms(x, mix_gain_ref[...]).astype(BF16)
    merged = None
    for i, y_ref in enumerate((ya_ref, yb_ref, yc_ref)):
        gate = jax.nn.sigmoid(jnp.dot(h, w_gate_ref[:, i * d:(i + 1) * d], preferred_element_type=F32))
        term = gate * jnp.dot(y_ref[...].astype(BF16), w_branch_ref[i], preferred_element_type=F32)
        merged = term if merged is None else merged + term
    o_ref[...] = x + jnp.dot(merged.astype(BF16), w_out_ref[...], preferred_element_type=F32)


def _merge(x, ya, yb, yc, p, layer):
    n, d = x.shape
    res = lambda a: _resident(a.shape, layer)
    return pl.pallas_call(
        _merge_kernel,
        grid=(n // ROW_TILE,),
        in_specs=[_rows(d), _rows(ya.shape[1]), _rows(yb.shape[1]), _rows(yc.shape[1]),
                  res(p["mix_norm"]), res(p["w_gate"]), res(p["w_branch"]), res(p["w_out"])],
        out_specs=_rows(d),
        out_shape=jax.ShapeDtypeStruct((n, d), F32),
        compiler_params=_params("parallel"),
        name="merge",
    )(x, ya, yb, yc, p["mix_norm"], p["w_gate"], p["w_branch"], p["w_out"])


def _row(a):
    return a[:, None, :]


def _col(a):
    return a[:, :, None]


def _pad_lanes(a, lane0):
    return _row(jnp.pad(a, ((0, 0), (lane0, LANES - lane0 - a.shape[1]))))


def kernel(x, positions, ffn1_norm, ffn1_w_in, ffn1_w_out, mix_norm, w_in, gm_v_norm, gm_w_s, gm_b_s, mla_q_norm, mla_kv_norm, mla_w_uq, mla_w_ukv, mla_q_gain, mla_k_gain, ssd_conv_w, ssd_conv_b, ssd_dt_bias, ssd_a_log, ssd_d, ssd_norm, w_branch, w_out, ffn2_norm, ffn2_w_in, ffn2_w_out):
    b, s, d = x.shape
    depth = w_in.shape[0]
    n = b * s
    gm_w = gm_v_norm.shape[1]
    q_rank = mla_q_norm.shape[1]
    kv_rank = mla_kv_norm.shape[1]
    ssd_inner = ssd_norm.shape[1]
    conv_dim = ssd_conv_b.shape[1]
    widths = (gm_w, q_rank, kv_rank, ssd_inner, conv_dim)

    sizes = (2 * gm_w, q_rank, kv_rank, MLA_ROPE, ssd_inner, conv_dim, SSD_HEADS, N_BRANCH * d)
    offs = [0]
    for w in sizes:
        offs.append(offs[-1] + w)
    col = lambda i: w_in[:, :, offs[i]:offs[i + 1]]
    w_small = jnp.zeros((depth, d, LANES), w_in.dtype)
    w_small = w_small.at[:, :, DT_LANE0:DT_LANE0 + SSD_HEADS].set(col(6))
    w_small = w_small.at[:, :, ROPE_LANE0:ROPE_LANE0 + MLA_ROPE].set(col(3))
    w_proj = jnp.concatenate([col(0), col(1), col(2), w_small, col(4), col(5)], axis=-1).astype(BF16)

    w_ukv = mla_w_ukv.reshape(depth, kv_rank, MLA_HEADS, MLA_NOPE + MLA_V)
    out_major = lambda w: w.reshape(depth, kv_rank, -1).transpose(0, 2, 1).astype(BF16)
    p = {
        "mix_norm": _row(mix_norm),
        "w_proj": w_proj,
        "w_gate": col(7).astype(BF16),
        "gm_gain": _row(gm_v_norm),
        "gm_w": gm_w_s,
        "gm_b": jnp.repeat(gm_b_s.transpose(0, 2, 1), gm_w // GM_GROUPS, axis=2),
        "q_norm": _col(mla_q_norm),
        "kv_norm": _col(mla_kv_norm),
        "w_uq": mla_w_uq.transpose(0, 2, 1).astype(BF16),
        "w_uk": out_major(w_ukv[..., :MLA_NOPE]),
        "w_uv": out_major(w_ukv[..., MLA_NOPE:]),
        "q_gain": _col(mla_q_gain * (MLA_QK_DIM ** -0.5 * LOG2_E)),
        "k_gain": _col(mla_k_gain),
        "conv_w": ssd_conv_w,
        "conv_b": _row(ssd_conv_b),
        "dt_bias": _pad_lanes(ssd_dt_bias, DT_LANE0),
        "a_log": _pad_lanes(ssd_a_log, DT_LANE0),
        "d_skip": _row(jnp.repeat(ssd_d, SSD_HEAD_DIM, axis=1)),
        "ssd_gain": _row(ssd_norm),
        "w_branch": w_branch.astype(BF16),
        "w_out": w_out.astype(BF16),
    }
    ffn = [(_row(ffn1_norm), ffn1_w_in.astype(BF16), ffn1_w_out.astype(BF16)),
           (_row(ffn2_norm), ffn2_w_in.astype(BF16), ffn2_w_out.astype(BF16))]

    cosf, sinf = _rope_tables(positions)
    xf = x.reshape(n, d)
    for layer in range(depth):
        xf = _ffn(xf, *ffn[0], layer)
        ya, qt, k, vt, small, z, xbc = _proj(xf, cosf, sinf, p, layer, widths)
        yb = _attention(qt, k, vt, b, s)
        yc = _ssd(xbc, z, small, p, layer, b, s)
        xf = _merge(xf, ya, yb, yc, p, layer)
        xf = _ffn(xf, *ffn[1], layer)
    return xf.reshape(b, s, d)
```

```python
import functools

import jax
import jax.numpy as jnp
from jax import lax
from jax.experimental import pallas as pl
from jax.experimental.pallas import tpu as pltpu

F32 = jnp.float32
BF16 = jnp.bfloat16

LANES = 128
VMEM_LIMIT_BYTES = 56 * 1024 * 1024

EPS = 1e-6
FFN_RESID = 0.5
GM_GROUPS = 4
GM_CHUNK = 128
MLA_HEADS = 8
MLA_NOPE = 64
MLA_ROPE = 32
MLA_QK_DIM = MLA_NOPE + MLA_ROPE
MLA_V = 64
ROPE_THETA = 10000.0
LOG2_E = 1.4426950408889634
SSD_HEADS = 8
SSD_HEAD_DIM = 64
SSD_GROUPS = 2
SSD_STATE = 128
SSD_CONV = 4
SSD_CHUNK = 128
N_BRANCH = 3
DT_LANE0 = 0
ROPE_LANE0 = MLA_NOPE

ROW_TILE = 512
FFN_TILE = 1024
ATTN_KSUB = 128
ATTN_QSUB = 256
ATTN_AHEAD = 6
ATTN_SUM_ROWS = 16
ATTN_KGROUP = 1
SSD_TILE = 512
FF_CHUNK = 256


def _rms(x, gain):
    ms = jnp.mean(x * x, axis=-1, keepdims=True)
    return x * lax.rsqrt(ms + EPS) * gain


def _gelu(x):
    return 0.5 * x * (1.0 + lax.erf(x * (2.0 ** -0.5)))


def _resident(shape, layer):
    zeros = (0,) * (len(shape) - 1)
    return pl.BlockSpec((None,) + tuple(shape[1:]), lambda *_: (layer,) + zeros, pipeline_mode=pl.Buffered(1))


def _rows(width, tile=ROW_TILE):
    return pl.BlockSpec((tile, width), lambda i: (i, 0))


def _params(*sem):
    return pltpu.CompilerParams(dimension_semantics=sem, vmem_limit_bytes=VMEM_LIMIT_BYTES)


def _rope_table_kernel(pos_ref, freq_ref, cos_ref, sin_ref):
    ang = freq_ref[...] * pos_ref[...].astype(F32)
    cos_ref[...] = jnp.cos(ang)
    sin_ref[...] = jnp.sin(ang)


def _rope_tables(positions):
    n = positions.size
    half = MLA_ROPE // 2
    inv_freq = 1.0 / (ROPE_THETA ** (jnp.arange(0, MLA_ROPE, 2, dtype=F32) / MLA_ROPE))
    tile = min(4096, n)
    return pl.pallas_call(
        _rope_table_kernel,
        grid=(n // tile,),
        in_specs=[pl.BlockSpec((1, tile), lambda i: (0, i)),
                  pl.BlockSpec((half, 1), lambda i: (0, 0))],
        out_specs=[pl.BlockSpec((half, tile), lambda i: (0, i))] * 2,
        out_shape=[jax.ShapeDtypeStruct((half, n), F32)] * 2,
        compiler_params=_params("parallel"),
        name="rope_tables",
    )(positions.reshape(1, n), inv_freq.reshape(half, 1))


def _ffn_body(x, gain, w_in_ref, w_out_ref, hid_ref):
    d_ff = w_out_ref.shape[0]
    h = _rms(x, gain).astype(BF16)
    for c in range(d_ff // FF_CHUNK):
        lo = c * FF_CHUNK
        g = jnp.dot(h, w_in_ref[:, lo:lo + FF_CHUNK], preferred_element_type=F32)
        u = jnp.dot(h, w_in_ref[:, d_ff + lo:d_ff + lo + FF_CHUNK], preferred_element_type=F32)
        hid_ref[:, lo:lo + FF_CHUNK] = (g * jax.nn.sigmoid(g) * u).astype(BF16)
    y = jnp.dot(hid_ref[...], w_out_ref[...], preferred_element_type=F32)
    return x + FFN_RESID * y


def _ffn_kernel(x_ref, gain_ref, w_in_ref, w_out_ref, o_ref, hid_ref):
    o_ref[...] = _ffn_body(x_ref[...], gain_ref[...], w_in_ref, w_out_ref, hid_ref)


def _ffn(x, gain, w_in, w_out, layer):
    n, d = x.shape
    d_ff = w_out.shape[1]
    assert d_ff % FF_CHUNK == 0 and n % FFN_TILE == 0
    return pl.pallas_call(
        _ffn_kernel,
        grid=(n // FFN_TILE,),
        in_specs=[_rows(d, FFN_TILE), _resident(gain.shape, layer), _resident(w_in.shape, layer),
                  _resident(w_out.shape, layer)],
        out_specs=_rows(d, FFN_TILE),
        out_shape=jax.ShapeDtypeStruct((n, d), F32),
        scratch_shapes=[pltpu.VMEM((FFN_TILE, d_ff), BF16)],
        compiler_params=_params("parallel"),
        name="ffn",
    )(x, gain, w_in, w_out)


def _lane_iota(shape):
    return lax.broadcasted_iota(jnp.int32, shape, len(shape) - 1)


def _head_norm_rope(x, gain, cos_t, sin_t):
    half = MLA_ROPE // 2
    ms = jnp.sum(x * x, axis=0, keepdims=True) * (1.0 / MLA_QK_DIM)
    y = x * lax.rsqrt(ms + EPS) * gain
    x1 = y[MLA_NOPE:MLA_NOPE + half]
    x2 = y[MLA_NOPE + half:MLA_QK_DIM]
    pad = jnp.zeros((LANES - MLA_QK_DIM, x.shape[1]), F32)
    return jnp.concatenate([y[:MLA_NOPE], x1 * cos_t - x2 * sin_t, x2 * cos_t + x1 * sin_t, pad], axis=0)


def _rms_rows(x, gain):
    ms = jnp.mean(x * x, axis=0, keepdims=True)
    return x * lax.rsqrt(ms + EPS) * gain


def _proj_kernel(x_ref, cos_ref, sin_ref, mix_gain_ref, w_lat_ref, w_small_ref, w_ssd_ref,
                 gm_gain_ref, gm_w_ref, gm_b_ref,
                 q_norm_ref, kv_norm_ref, w_uq_ref, w_uk_ref, w_uv_ref, q_gain_ref, k_gain_ref,
                 ya_ref, qt_ref, k_ref, vt_ref, small_ref, z_ref, xbc_ref,
                 *, widths):
    gm_w, q_rank, kv_rank, ssd_inner, conv_dim = widths
    tile = x_ref.shape[0]
    h = _rms(x_ref[...], mix_gain_ref[...]).astype(BF16)

    def proj(w_ref, lo, width):
        return jnp.dot(h, w_ref[:, lo:lo + width], preferred_element_type=F32)

    off_cq = 2 * gm_w
    off_ckv = off_cq + q_rank

    u_raw = proj(w_lat_ref, 0, gm_w)
    v_raw = proj(w_lat_ref, gm_w, gm_w)
    c_q = proj(w_lat_ref, off_cq, q_rank)
    c_kv = proj(w_lat_ref, off_ckv, kv_rank)
    small = proj(w_small_ref, 0, LANES)
    small_ref[...] = small

    u = _gelu(u_raw)
    v = _gelu(v_raw)
    v = _rms(v, gm_gain_ref[...]).astype(BF16)
    n_chunk = tile // GM_CHUNK
    gdim = gm_w // GM_GROUPS
    row = lax.broadcasted_iota(jnp.int32, (GM_CHUNK, GM_CHUNK), 0)
    col = lax.broadcasted_iota(jnp.int32, (GM_CHUNK, GM_CHUNK), 1)
    for g in range(GM_GROUPS):
        w = jnp.where(col <= row, gm_w_ref[g], 0.0).astype(BF16)
        rhs = jnp.concatenate([v[c * GM_CHUNK:(c + 1) * GM_CHUNK, g * gdim:(g + 1) * gdim] for c in range(n_chunk)], axis=1)
        sp = jnp.dot(w, rhs, preferred_element_type=F32)
        bias = gm_b_ref[:, g * gdim:(g + 1) * gdim]
        for c in range(n_chunk):
            rows = slice(c * GM_CHUNK, (c + 1) * GM_CHUNK)
            ya_ref[rows, g * gdim:(g + 1) * gdim] = (
                u[rows, g * gdim:(g + 1) * gdim] * (sp[:, c * gdim:(c + 1) * gdim] + bias)).astype(BF16)

    z_ref[...] = proj(w_ssd_ref, 0, ssd_inner).astype(BF16)
    lat_t = jnp.concatenate([c_q, c_kv, small], axis=1).T
    qn_t = _rms_rows(lat_t[:q_rank], q_norm_ref[...]).astype(BF16)
    kvn_t = _rms_rows(lat_t[q_rank:q_rank + kv_rank], kv_norm_ref[...]).astype(BF16)
    k_rope_t = lat_t[q_rank + kv_rank + ROPE_LANE0:q_rank + kv_rank + ROPE_LANE0 + MLA_ROPE]
    cos_t = cos_ref[...]
    sin_t = sin_ref[...]
    vt_ref[...] = jnp.dot(w_uv_ref[...], kvn_t, preferred_element_type=F32).astype(BF16)
    q_all = jnp.dot(w_uq_ref[...], qn_t, preferred_element_type=F32)
    k_nope = jnp.dot(w_uk_ref[...], kvn_t, preferred_element_type=F32)
    xbc_ref[...] = proj(w_ssd_ref, ssd_inner, conv_dim).astype(BF16)
    q_gain = q_gain_ref[...]
    k_gain = k_gain_ref[...]
    for hd in range(MLA_HEADS):
        qh = q_all[hd * MLA_QK_DIM:(hd + 1) * MLA_QK_DIM]
        qt_ref[hd * LANES:(hd + 1) * LANES, :] = _head_norm_rope(qh, q_gain, cos_t, sin_t).astype(BF16)
        kh = jnp.concatenate([k_nope[hd * MLA_NOPE:(hd + 1) * MLA_NOPE], k_rope_t], axis=0)
        k_ref[:, hd * LANES:(hd + 1) * LANES] = _head_norm_rope(kh, k_gain, cos_t, sin_t).T.astype(BF16)


def _proj(x, cosf, sinf, p, layer, widths):
    n, d = x.shape
    gm_w, q_rank, kv_rank, ssd_inner, conv_dim = widths
    head_w = MLA_HEADS * LANES
    v_w = MLA_HEADS * MLA_V
    tiles = n // ROW_TILE
    res = lambda a: _resident(a.shape, layer)
    table = pl.BlockSpec((MLA_ROPE // 2, ROW_TILE), lambda i: (0, i))
    chan_major = lambda c: pl.BlockSpec((None, c, ROW_TILE), lambda i: (i, 0, 0))
    return pl.pallas_call(
        functools.partial(_proj_kernel, widths=widths),
        grid=(tiles,),
        in_specs=[_rows(d), table, table, res(p["mix_norm"]), res(p["w_lat"]), res(p["w_small"]), res(p["w_ssd"]),
                  res(p["gm_gain"]), res(p["gm_w"]), res(p["gm_b"]),
                  res(p["q_norm"]), res(p["kv_norm"]), res(p["w_uq"]), res(p["w_uk"]), res(p["w_uv"]),
                  res(p["q_gain"]), res(p["k_gain"])],
        out_specs=[_rows(gm_w), chan_major(head_w), _rows(head_w), chan_major(v_w),
                   _rows(LANES), _rows(ssd_inner), _rows(conv_dim)],
        out_shape=[jax.ShapeDtypeStruct((n, gm_w), BF16),
                   jax.ShapeDtypeStruct((tiles, head_w, ROW_TILE), BF16),
                   jax.ShapeDtypeStruct((n, head_w), BF16),
                   jax.ShapeDtypeStruct((tiles, v_w, ROW_TILE), BF16),
                   jax.ShapeDtypeStruct((n, LANES), F32),
                   jax.ShapeDtypeStruct((n, ssd_inner), BF16),
                   jax.ShapeDtypeStruct((n, conv_dim), BF16)],
        compiler_params=_params("parallel"),
        name="proj",
    )(x, cosf, sinf, p["mix_norm"], p["w_lat"], p["w_small"], p["w_ssd"], p["gm_gain"], p["gm_w"], p["gm_b"],
      p["q_norm"], p["kv_norm"], p["w_uq"], p["w_uk"], p["w_uv"], p["q_gain"], p["k_gain"])


def _attn_kernel(qt_ref, k_ref, vt_ref, o_ref, m_ref, acc_ref):
    n_t, _, t = qt_ref.shape
    seq = n_t * t

    m_ref[...] = jnp.full(m_ref.shape, -jnp.inf, F32)
    acc_ref[...] = jnp.zeros(acc_ref.shape, F32)
    ones = jnp.ones((ATTN_SUM_ROWS, ATTN_KGROUP * ATTN_KSUB), BF16)

    work = []
    for kg in range(0, seq, ATTN_KGROUP * ATTN_KSUB):
        for hh in range(2):
            for q0 in range(0, seq, ATTN_QSUB):
                subs = [(k0, k0 + ATTN_KSUB - 1 > q0) for k0 in range(kg, kg + ATTN_KGROUP * ATTN_KSUB, ATTN_KSUB)
                        if k0 <= q0 + ATTN_QSUB - 1]
                if subs:
                    work.append((hh, q0, subs))
    tiles = [(hh, k0, q0) for hh, q0, subs in work for k0, _ in subs]

    def scores(hh, k0, q0):
        q_sub = qt_ref[q0 // t, hh * LANES:(hh + 1) * LANES, q0 % t:q0 % t + ATTN_QSUB]
        k_sub = k_ref[k0 // t, k0 % t:k0 % t + ATTN_KSUB, hh * LANES:(hh + 1) * LANES]
        return jnp.dot(k_sub, q_sub, preferred_element_type=F32)

    issued = min(ATTN_AHEAD, len(tiles))
    ahead = [scores(*x) for x in tiles[:issued]]
    for hh, q0, subs in work:
        s_subs = []
        for k0, masked in subs:
            s = ahead.pop(0)
            if issued < len(tiles):
                ahead.append(scores(*tiles[issued]))
                issued += 1
            if masked:
                kpos = k0 + lax.broadcasted_iota(jnp.int32, s.shape, 0)
                qpos = q0 + lax.broadcasted_iota(jnp.int32, s.shape, 1)
                s = jnp.where(kpos <= qpos, s, -jnp.inf)
            s_subs.append(s)
        qcols = slice(q0, q0 + ATTN_QSUB)
        m = m_ref[hh, :, qcols]
        m_new = m
        for s in s_subs:
            m_new = jnp.maximum(m_new, jnp.max(s, axis=0, keepdims=True))
        alpha = jnp.exp2(m - m_new)
        p = jnp.concatenate([jnp.exp2((s - m_new).astype(BF16)) for s in s_subs], axis=0)
        m_ref[hh, :, qcols] = m_new
        k_lo = subs[0][0]
        width = len(subs) * ATTN_KSUB
        v_sub = vt_ref[k_lo // t, hh * MLA_V:(hh + 1) * MLA_V, k_lo % t:k_lo % t + width]
        pv = jnp.dot(jnp.concatenate([v_sub, ones[:, :width]], axis=0), p, preferred_element_type=F32)
        acc_ref[hh, :, qcols] = alpha * acc_ref[hh, :, qcols] + pv

    out = jnp.concatenate([acc_ref[hh, :MLA_V] / acc_ref[hh, MLA_V:MLA_V + 1] for hh in range(2)], axis=0)
    o_ref[...] = out.T.astype(BF16)


def _attention(qt, k, vt, b, s):
    tiles, _, t = qt.shape
    pairs = MLA_HEADS // 2
    per_b = s // t
    assert tiles == b * per_b and t % (ATTN_KGROUP * ATTN_KSUB) == 0 and t % ATTN_QSUB == 0
    return pl.pallas_call(
        _attn_kernel,
        grid=(b, pairs),
        in_specs=[pl.BlockSpec((per_b, 2 * LANES, t), lambda bi, j: (bi, j, 0)),
                  pl.BlockSpec((per_b, t, 2 * LANES), lambda bi, j: (bi, 0, j)),
                  pl.BlockSpec((per_b, 2 * MLA_V, t), lambda bi, j: (bi, j, 0))],
        out_specs=pl.BlockSpec((s, 2 * MLA_V), lambda bi, j: (bi, j)),
        out_shape=jax.ShapeDtypeStruct((b * s, MLA_HEADS * MLA_V), BF16),
        scratch_shapes=[pltpu.VMEM((2, 1, s), F32), pltpu.VMEM((2, MLA_V + ATTN_SUM_ROWS, s), F32)],
        compiler_params=_params("parallel", "parallel"),
        name="attn",
    )(qt, k.reshape(tiles, t, k.shape[1]), vt)


def _cumsum_lanes(x):
    lane = _lane_iota(x.shape)
    sh = 1
    while sh < x.shape[-1]:
        x = x + jnp.where(lane >= sh, pltpu.roll(x, sh, 1), 0.0)
        sh *= 2
    return x


def _ssd_kernel(xbc_ref, z_ref, small_ref, conv_w_ref, conv_b_ref, dt_bias_ref, a_log_ref, d_skip_ref, gain_ref,
                y_ref, ext_ref, state_ref):
    tile = xbc_ref.shape[0]
    inner = z_ref.shape[1]
    gstate = SSD_GROUPS * SSD_STATE

    @pl.when(pl.program_id(1) == 0)
    def _():
        ext_ref[0:SSD_CHUNK, :] = jnp.zeros((SSD_CHUNK, ext_ref.shape[1]), BF16)
        state_ref[...] = jnp.zeros(state_ref.shape, F32)

    ext_ref[SSD_CHUNK:SSD_CHUNK + tile, :] = xbc_ref[...]

    n_shift = SSD_CONV - 1
    srow = lax.broadcasted_iota(jnp.int32, (SSD_CHUNK, 2 * SSD_CHUNK), 0)
    scol = lax.broadcasted_iota(jnp.int32, (SSD_CHUNK, 2 * SSD_CHUNK), 1)
    pick = jnp.concatenate([jnp.where(scol == SSD_CHUNK + srow - (d + 1), 1.0, 0.0).astype(BF16)
                            for d in range(n_shift)], axis=0)

    def conv_chunk(c):
        window = ext_ref[c * SSD_CHUNK:(c + 2) * SSD_CHUNK, :]
        shifted = jnp.dot(pick, window, preferred_element_type=F32)
        conv = conv_b_ref[...] + conv_w_ref[n_shift:n_shift + 1, :] * window[SSD_CHUNK:].astype(F32)
        for d in range(n_shift):
            conv = conv + conv_w_ref[n_shift - 1 - d:n_shift - d, :] * shifted[d * SSD_CHUNK:(d + 1) * SSD_CHUNK]
        return conv * jax.nn.sigmoid(conv)

    lane = _lane_iota((tile, LANES))
    head_lane = (lane >= DT_LANE0) & (lane < DT_LANE0 + SSD_HEADS)
    dt = jnp.where(head_lane, jax.nn.softplus(small_ref[...] + dt_bias_ref[...]), 0.0)
    da = dt * (-jnp.exp(a_log_ref[...]))

    sub = lax.broadcasted_iota(jnp.int32, (SSD_CHUNK, SSD_CHUNK), 0)
    lan = lax.broadcasted_iota(jnp.int32, (SSD_CHUNK, SSD_CHUNK), 1)
    causal = lan <= sub
    low_half = lan < SSD_HEAD_DIM
    heads_per_group = SSD_HEADS // SSD_GROUPS

    xc_next = conv_chunk(0)
    for c in range(tile // SSD_CHUNK):
        rows = slice(c * SSD_CHUNK, (c + 1) * SSD_CHUNK)
        cs_row = _cumsum_lanes(da[rows].T[DT_LANE0:DT_LANE0 + SSD_HEADS])
        cs_col = jnp.concatenate([cs_row, jnp.zeros((SSD_CHUNK - SSD_HEADS, SSD_CHUNK), F32)], axis=0).T
        dt_c = dt[rows]
        xc = xc_next
        if c + 1 < tile // SSD_CHUNK:
            xc_next = conv_chunk(c + 1)
        b_t = [xc[:, inner + g * SSD_STATE:inner + (g + 1) * SSD_STATE].T.astype(BF16) for g in range(SSD_GROUPS)]
        c_in = [xc[:, inner + gstate + g * SSD_STATE:inner + gstate + (g + 1) * SSD_STATE].astype(BF16)
                for g in range(SSD_GROUPS)]
        cb = [jnp.dot(c_in[g], b_t[g], preferred_element_type=F32) for g in range(SSD_GROUPS)]
        for pr in range(SSD_HEADS // 2):
            g = (2 * pr) // heads_per_group
            lanes = slice(pr * LANES, (pr + 1) * LANES)
            xs = xc[:, lanes]
            col = [cs_col[:, 2 * pr + i:2 * pr + i + 1] for i in range(2)]
            row = [cs_row[2 * pr + i:2 * pr + i + 1, :] for i in range(2)]
            dtc = [dt_c[:, 2 * pr + i:2 * pr + i + 1] for i in range(2)]
            tot = [cs_col[SSD_CHUNK - 1:SSD_CHUNK, 2 * pr + i:2 * pr + i + 1] for i in range(2)]
            xdt = xs * jnp.where(low_half, dtc[0], dtc[1])
            xdt_b = xdt.astype(BF16)
            y_d = []
            for i in range(2):
                lmat = jnp.exp(jnp.where(causal, col[i] - row[i], -jnp.inf))
                y_d.append(jnp.dot((cb[g] * lmat).astype(BF16), xdt_b, preferred_element_type=F32))
            y_diag = jnp.where(low_half, y_d[0], y_d[1])
            decay = jnp.where(low_half, jnp.exp(tot[0] - col[0]), jnp.exp(tot[1] - col[1]))
            states = jnp.dot(b_t[g], (xdt * decay).astype(BF16), preferred_element_type=F32)
            s_in = state_ref[pr]
            y_off = jnp.dot(c_in[g], s_in.astype(BF16), preferred_element_type=F32)
            y_off = y_off * jnp.where(low_half, jnp.exp(col[0]), jnp.exp(col[1]))
            state_ref[pr] = s_in * jnp.where(low_half, jnp.exp(tot[0]), jnp.exp(tot[1])) + states
            y = y_diag + y_off + xs * d_skip_ref[:, lanes]
            zz = z_ref[rows, lanes].astype(F32)
            y_ref[rows, lanes] = y * (zz * jax.nn.sigmoid(zz))

    ext_ref[0:SSD_CHUNK, :] = ext_ref[tile:tile + SSD_CHUNK, :]

    gw = inner // SSD_GROUPS
    for g in range(SSD_GROUPS):
        lanes = slice(g * gw, (g + 1) * gw)
        y_ref[:, lanes] = _rms(y_ref[:, lanes], gain_ref[:, lanes])


def _ssd(xbc, z, small, p, layer, b, s):
    n, conv_dim = xbc.shape
    inner = z.shape[1]
    tile = min(SSD_TILE, s)
    assert s % tile == 0 and tile % SSD_CHUNK == 0
    per_b = s // tile
    rows = lambda w: pl.BlockSpec((tile, w), lambda bi, si: (bi * per_b + si, 0))
    res = lambda a: _resident(a.shape, layer)
    return pl.pallas_call(
        _ssd_kernel,
        grid=(b, per_b),
        in_specs=[rows(conv_dim), rows(inner), rows(LANES), res(p["conv_w"]), res(p["conv_b"]), res(p["dt_bias"]),
                  res(p["a_log"]), res(p["d_skip"]), res(p["ssd_gain"])],
        out_specs=rows(inner),
        out_shape=jax.ShapeDtypeStruct((n, inner), F32),
        scratch_shapes=[pltpu.VMEM((SSD_CHUNK + tile, conv_dim), BF16),
                        pltpu.VMEM((SSD_HEADS // 2, SSD_STATE, 2 * SSD_HEAD_DIM), F32)],
        compiler_params=_params("parallel", "arbitrary"),
        name="ssd",
    )(xbc, z, small, p["conv_w"], p["conv_b"], p["dt_bias"], p["a_log"], p["d_skip"], p["ssd_gain"])


def _merge_kernel(x_ref, ya_ref, yb_ref, yc_ref, mix_gain_ref, w_gate_ref, w_branch_ref, w_out_ref, o_ref):
    x = x_ref[...]
    d = x.shape[1]
    h = _rms(x, mix_gain_ref[...]).astype(BF16)
    merged = None
    for i, y_ref in enumerate((ya_ref, yb_ref, yc_ref)):
        gate = jax.nn.sigmoid(jnp.dot(h, w_gate_ref[:, i * d:(i + 1) * d], preferred_element_type=F32))
        term = gate * jnp.dot(y_ref[...].astype(BF16), w_branch_ref[i], preferred_element_type=F32)
        merged = term if merged is None else merged + term
    o_ref[...] = x + jnp.dot(merged.astype(BF16), w_out_ref[...], preferred_element_type=F32)


def _merge(x, ya, yb, yc, p, layer):
    n, d = x.shape
    res = lambda a: _resident(a.shape, layer)
    return pl.pallas_call(
        _merge_kernel,
        grid=(n // ROW_TILE,),
        in_specs=[_rows(d), _rows(ya.shape[1]), _rows(yb.shape[1]), _rows(yc.shape[1]),
                  res(p["mix_norm"]), res(p["w_gate"]), res(p["w_branch"]), res(p["w_out"])],
        out_specs=_rows(d),
        out_shape=jax.ShapeDtypeStruct((n, d), F32),
        compiler_params=_params("parallel"),
        name="merge",
    )(x, ya, yb, yc, p["mix_norm"], p["w_gate"], p["w_branch"], p["w_out"])


def _row(a):
    return a[:, None, :]


def _col(a):
    return a[:, :, None]


def _pad_lanes(a, lane0):
    return _row(jnp.pad(a, ((0, 0), (lane0, LANES - lane0 - a.shape[1]))))


def kernel(x, positions, ffn1_norm, ffn1_w_in, ffn1_w_out, mix_norm, w_in, gm_v_norm, gm_w_s, gm_b_s, mla_q_norm, mla_kv_norm, mla_w_uq, mla_w_ukv, mla_q_gain, mla_k_gain, ssd_conv_w, ssd_conv_b, ssd_dt_bias, ssd_a_log, ssd_d, ssd_norm, w_branch, w_out, ffn2_norm, ffn2_w_in, ffn2_w_out):
    b, s, d = x.shape
    depth = w_in.shape[0]
    n = b * s
    gm_w = gm_v_norm.shape[1]
    q_rank = mla_q_norm.shape[1]
    kv_rank = mla_kv_norm.shape[1]
    ssd_inner = ssd_norm.shape[1]
    conv_dim = ssd_conv_b.shape[1]
    widths = (gm_w, q_rank, kv_rank, ssd_inner, conv_dim)

    sizes = (2 * gm_w, q_rank, kv_rank, MLA_ROPE, ssd_inner, conv_dim, SSD_HEADS, N_BRANCH * d)
    offs = [0]
    for w in sizes:
        offs.append(offs[-1] + w)
    cols = lambda i, j: w_in[:, :, offs[i]:offs[j]].astype(BF16)
    w_small = jnp.zeros((depth, d, LANES), BF16)
    w_small = w_small.at[:, :, DT_LANE0:DT_LANE0 + SSD_HEADS].set(cols(6, 7))
    w_small = w_small.at[:, :, ROPE_LANE0:ROPE_LANE0 + MLA_ROPE].set(cols(3, 4))

    w_ukv = mla_w_ukv.reshape(depth, kv_rank, MLA_HEADS, MLA_NOPE + MLA_V)
    out_major = lambda w: w.reshape(depth, kv_rank, -1).transpose(0, 2, 1).astype(BF16)
    p = {
        "mix_norm": _row(mix_norm),
        "w_lat": cols(0, 3),
        "w_small": w_small,
        "w_ssd": cols(4, 6),
        "w_gate": cols(7, 8),
        "gm_gain": _row(gm_v_norm),
        "gm_w": gm_w_s,
        "gm_b": jnp.repeat(gm_b_s.transpose(0, 2, 1), gm_w // GM_GROUPS, axis=2),
        "q_norm": _col(mla_q_norm),
        "kv_norm": _col(mla_kv_norm),
        "w_uq": mla_w_uq.transpose(0, 2, 1).astype(BF16),
        "w_uk": out_major(w_ukv[..., :MLA_NOPE]),
        "w_uv": out_major(w_ukv[..., MLA_NOPE:]),
        "q_gain": _col(mla_q_gain * (MLA_QK_DIM ** -0.5 * LOG2_E)),
        "k_gain": _col(mla_k_gain),
        "conv_w": ssd_conv_w,
        "conv_b": _row(ssd_conv_b),
        "dt_bias": _pad_lanes(ssd_dt_bias, DT_LANE0),
        "a_log": _pad_lanes(ssd_a_log, DT_LANE0),
        "d_skip": _row(jnp.repeat(ssd_d, SSD_HEAD_DIM, axis=1)),
        "ssd_gain": _row(ssd_norm),
        "w_branch": w_branch.astype(BF16),
        "w_out": w_out.astype(BF16),
    }
    ffn = [(_row(ffn1_norm), ffn1_w_in.astype(BF16), ffn1_w_out.astype(BF16)),
           (_row(ffn2_norm), ffn2_w_in.astype(BF16), ffn2_w_out.astype(BF16))]

    cosf, sinf = _rope_tables(positions)
    xf = x.reshape(n, d)
    for layer in range(depth):
        xf = _ffn(xf, *ffn[0], layer)
        ya, qt, k, vt, small, z, xbc = _proj(xf, cosf, sinf, p, layer, widths)
        yb = _attention(qt, k, vt, b, s)
        yc = _ssd(xbc, z, small, p, layer, b, s)
        xf = _merge(xf, ya, yb, yc, p, layer)
        xf = _ffn(xf, *ffn[1], layer)
    return xf.reshape(b, s, d)
```

```python
import functools

import jax
import jax.numpy as jnp
from jax import lax
from jax.experimental import pallas as pl
from jax.experimental.pallas import tpu as pltpu

F32 = jnp.float32
BF16 = jnp.bfloat16

LANES = 128
VMEM_LIMIT_BYTES = 56 * 1024 * 1024

EPS = 1e-6
FFN_RESID = 0.5
GM_GROUPS = 4
GM_CHUNK = 128
MLA_HEADS = 8
MLA_NOPE = 64
MLA_ROPE = 32
MLA_QK_DIM = MLA_NOPE + MLA_ROPE
MLA_V = 64
ROPE_THETA = 10000.0
LOG2_E = 1.4426950408889634
SSD_HEADS = 8
SSD_HEAD_DIM = 64
SSD_GROUPS = 2
SSD_STATE = 128
SSD_CONV = 4
SSD_CHUNK = 128
N_BRANCH = 3
DT_LANE0 = 0
ROPE_LANE0 = MLA_NOPE

ROW_TILE = 512
FFN_TILE = 1024
ATTN_KSUB = 128
ATTN_QSUB = 256
ATTN_AHEAD = 6
ATTN_SUM_ROWS = 16
ATTN_KGROUP = 1
SSD_TILE = 512
FF_CHUNK = 256


def _rms(x, gain):
    ms = jnp.mean(x * x, axis=-1, keepdims=True)
    return x * lax.rsqrt(ms + EPS) * gain


def _gelu(x):
    return 0.5 * x * (1.0 + lax.erf(x * (2.0 ** -0.5)))


def _resident(shape, layer):
    zeros = (0,) * (len(shape) - 1)
    return pl.BlockSpec((None,) + tuple(shape[1:]), lambda *_: (layer,) + zeros, pipeline_mode=pl.Buffered(1))


def _rows(width, tile=ROW_TILE):
    return pl.BlockSpec((tile, width), lambda i: (i, 0))


def _params(*sem):
    return pltpu.CompilerParams(dimension_semantics=sem, vmem_limit_bytes=VMEM_LIMIT_BYTES)


def _rope_table_kernel(pos_ref, freq_ref, cos_ref, sin_ref):
    ang = freq_ref[...] * pos_ref[...].astype(F32)
    cos_ref[...] = jnp.cos(ang)
    sin_ref[...] = jnp.sin(ang)


def _rope_tables(positions):
    n = positions.size
    half = MLA_ROPE // 2
    inv_freq = 1.0 / (ROPE_THETA ** (jnp.arange(0, MLA_ROPE, 2, dtype=F32) / MLA_ROPE))
    tile = min(4096, n)
    return pl.pallas_call(
        _rope_table_kernel,
        grid=(n // tile,),
        in_specs=[pl.BlockSpec((1, tile), lambda i: (0, i)),
                  pl.BlockSpec((half, 1), lambda i: (0, 0))],
        out_specs=[pl.BlockSpec((half, tile), lambda i: (0, i))] * 2,
        out_shape=[jax.ShapeDtypeStruct((half, n), F32)] * 2,
        compiler_params=_params("parallel"),
        name="rope_tables",
    )(positions.reshape(1, n), inv_freq.reshape(half, 1))


def _ffn_body(x, gain, w_in_ref, w_out_ref, hid_ref):
    d_ff = w_out_ref.shape[0]
    h = _rms(x, gain).astype(BF16)
    for c in range(d_ff // FF_CHUNK):
        lo = c * FF_CHUNK
        g = jnp.dot(h, w_in_ref[:, lo:lo + FF_CHUNK], preferred_element_type=F32)
        u = jnp.dot(h, w_in_ref[:, d_ff + lo:d_ff + lo + FF_CHUNK], preferred_element_type=F32)
        hid_ref[:, lo:lo + FF_CHUNK] = (g * jax.nn.sigmoid(g) * u).astype(BF16)
    y = jnp.dot(hid_ref[...], w_out_ref[...], preferred_element_type=F32)
    return x + FFN_RESID * y


def _ffn_kernel(x_ref, gain_ref, w_in_ref, w_out_ref, o_ref, hid_ref):
    o_ref[...] = _ffn_body(x_ref[...], gain_ref[...], w_in_ref, w_out_ref, hid_ref)


def _ffn(x, gain, w_in, w_out, layer):
    n, d = x.shape
    d_ff = w_out.shape[1]
    assert d_ff % FF_CHUNK == 0 and n % FFN_TILE == 0
    return pl.pallas_call(
        _ffn_kernel,
        grid=(n // FFN_TILE,),
        in_specs=[_rows(d, FFN_TILE), _resident(gain.shape, layer), _resident(w_in.shape, layer),
                  _resident(w_out.shape, layer)],
        out_specs=_rows(d, FFN_TILE),
        out_shape=jax.ShapeDtypeStruct((n, d), F32),
        scratch_shapes=[pltpu.VMEM((FFN_TILE, d_ff), BF16)],
        compiler_params=_params("parallel"),
        name="ffn",
    )(x, gain, w_in, w_out)


def _lane_iota(shape):
    return lax.broadcasted_iota(jnp.int32, shape, len(shape) - 1)


def _head_norm_rope(x, gain, cos_t, sin_t):
    half = MLA_ROPE // 2
    ms = jnp.sum(x * x, axis=0, keepdims=True) * (1.0 / MLA_QK_DIM)
    y = x * lax.rsqrt(ms + EPS) * gain
    x1 = y[MLA_NOPE:MLA_NOPE + half]
    x2 = y[MLA_NOPE + half:MLA_QK_DIM]
    pad = jnp.zeros((LANES - MLA_QK_DIM, x.shape[1]), F32)
    return jnp.concatenate([y[:MLA_NOPE], x1 * cos_t - x2 * sin_t, x2 * cos_t + x1 * sin_t, pad], axis=0)


def _rms_rows(x, gain):
    ms = jnp.mean(x * x, axis=0, keepdims=True)
    return x * lax.rsqrt(ms + EPS) * gain


def _proj_kernel(x_ref, cos_ref, sin_ref, mix_gain_ref, w_lat_ref, w_small_ref, w_ssd_ref,
                 gm_gain_ref, gm_w_ref, gm_b_ref,
                 q_norm_ref, kv_norm_ref, w_uq_ref, w_uk_ref, w_uv_ref, q_gain_ref, k_gain_ref,
                 ya_ref, qt_ref, k_ref, vt_ref, small_ref, z_ref, xbc_ref,
                 *, widths):
    gm_w, q_rank, kv_rank, ssd_inner, conv_dim = widths
    tile = x_ref.shape[0]
    h = _rms(x_ref[...], mix_gain_ref[...]).astype(BF16)

    def proj(w_ref, lo, width):
        return jnp.dot(h, w_ref[:, lo:lo + width], preferred_element_type=F32)

    off_cq = 2 * gm_w
    off_ckv = off_cq + q_rank

    u_raw = proj(w_lat_ref, 0, gm_w)
    v_raw = proj(w_lat_ref, gm_w, gm_w)
    c_q = proj(w_lat_ref, off_cq, q_rank)
    c_kv = proj(w_lat_ref, off_ckv, kv_rank)
    small = proj(w_small_ref, 0, LANES)
    small_ref[...] = small

    u = _gelu(u_raw)
    v = _gelu(v_raw)
    v = _rms(v, gm_gain_ref[...]).astype(BF16)
    n_chunk = tile // GM_CHUNK
    gdim = gm_w // GM_GROUPS
    row = lax.broadcasted_iota(jnp.int32, (GM_CHUNK, GM_CHUNK), 0)
    col = lax.broadcasted_iota(jnp.int32, (GM_CHUNK, GM_CHUNK), 1)
    for g in range(GM_GROUPS):
        w = jnp.where(col <= row, gm_w_ref[g], 0.0).astype(BF16)
        rhs = jnp.concatenate([v[c * GM_CHUNK:(c + 1) * GM_CHUNK, g * gdim:(g + 1) * gdim] for c in range(n_chunk)], axis=1)
        sp = jnp.dot(w, rhs, preferred_element_type=F32)
        bias = gm_b_ref[:, g * gdim:(g + 1) * gdim]
        for c in range(n_chunk):
            rows = slice(c * GM_CHUNK, (c + 1) * GM_CHUNK)
            ya_ref[rows, g * gdim:(g + 1) * gdim] = (
                u[rows, g * gdim:(g + 1) * gdim] * (sp[:, c * gdim:(c + 1) * gdim] + bias)).astype(BF16)

    z_ref[...] = proj(w_ssd_ref, 0, ssd_inner).astype(BF16)
    lat_t = jnp.concatenate([c_q, c_kv, small], axis=1).T
    qn_t = _rms_rows(lat_t[:q_rank], q_norm_ref[...]).astype(BF16)
    kvn_t = _rms_rows(lat_t[q_rank:q_rank + kv_rank], kv_norm_ref[...]).astype(BF16)
    k_rope_t = lat_t[q_rank + kv_rank + ROPE_LANE0:q_rank + kv_rank + ROPE_LANE0 + MLA_ROPE]
    cos_t = cos_ref[...]
    sin_t = sin_ref[...]
    vt_ref[...] = jnp.dot(w_uv_ref[...], kvn_t, preferred_element_type=F32).astype(BF16)
    q_all = jnp.dot(w_uq_ref[...], qn_t, preferred_element_type=F32)
    k_nope = jnp.dot(w_uk_ref[...], kvn_t, preferred_element_type=F32)
    xbc_ref[...] = proj(w_ssd_ref, ssd_inner, conv_dim).astype(BF16)
    q_gain = q_gain_ref[...]
    k_gain = k_gain_ref[...]
    for hd in range(MLA_HEADS):
        qh = q_all[hd * MLA_QK_DIM:(hd + 1) * MLA_QK_DIM]
        qt_ref[hd * LANES:(hd + 1) * LANES, :] = _head_norm_rope(qh, q_gain, cos_t, sin_t).astype(BF16)
        kh = jnp.concatenate([k_nope[hd * MLA_NOPE:(hd + 1) * MLA_NOPE], k_rope_t], axis=0)
        k_ref[:, hd * LANES:(hd + 1) * LANES] = _head_norm_rope(kh, k_gain, cos_t, sin_t).T.astype(BF16)


def _proj(x, cosf, sinf, p, layer, widths):
    n, d = x.shape
    gm_w, q_rank, kv_rank, ssd_inner, conv_dim = widths
    head_w = MLA_HEADS * LANES
    v_w = MLA_HEADS * MLA_V
    tiles = n // ROW_TILE
    res = lambda a: _resident(a.shape, layer)
    table = pl.BlockSpec((MLA_ROPE // 2, ROW_TILE), lambda i: (0, i))
    chan_major = lambda c: pl.BlockSpec((None, c, ROW_TILE), lambda i: (i, 0, 0))
    return pl.pallas_call(
        functools.partial(_proj_kernel, widths=widths),
        grid=(tiles,),
        in_specs=[_rows(d), table, table, res(p["mix_norm"]), res(p["w_lat"]), res(p["w_small"]), res(p["w_ssd"]),
                  res(p["gm_gain"]), res(p["gm_w"]), res(p["gm_b"]),
                  res(p["q_norm"]), res(p["kv_norm"]), res(p["w_uq"]), res(p["w_uk"]), res(p["w_uv"]),
                  res(p["q_gain"]), res(p["k_gain"])],
        out_specs=[_rows(gm_w), chan_major(head_w), _rows(head_w), chan_major(v_w),
                   _rows(LANES), _rows(ssd_inner), _rows(conv_dim)],
        out_shape=[jax.ShapeDtypeStruct((n, gm_w), BF16),
                   jax.ShapeDtypeStruct((tiles, head_w, ROW_TILE), BF16),
                   jax.ShapeDtypeStruct((n, head_w), BF16),
                   jax.ShapeDtypeStruct((tiles, v_w, ROW_TILE), BF16),
                   jax.ShapeDtypeStruct((n, LANES), F32),
                   jax.ShapeDtypeStruct((n, ssd_inner), BF16),
                   jax.ShapeDtypeStruct((n, conv_dim), BF16)],
        compiler_params=_params("parallel"),
        name="proj",
    )(x, cosf, sinf, p["mix_norm"], p["w_lat"], p["w_small"], p["w_ssd"], p["gm_gain"], p["gm_w"], p["gm_b"],
      p["q_norm"], p["kv_norm"], p["w_uq"], p["w_uk"], p["w_uv"], p["q_gain"], p["k_gain"])


def _attn_kernel(qt_ref, k_ref, vt_ref, o_ref, m_ref, acc_ref):
    n_t, _, t = qt_ref.shape
    seq = n_t * t

    m_ref[...] = jnp.full(m_ref.shape, -jnp.inf, F32)
    acc_ref[...] = jnp.zeros(acc_ref.shape, F32)
    ones = jnp.ones((ATTN_SUM_ROWS, ATTN_KGROUP * ATTN_KSUB), BF16)

    work = []
    for kg in range(0, seq, ATTN_KGROUP * ATTN_KSUB):
        for hh in range(2):
            for q0 in range(0, seq, ATTN_QSUB):
                subs = [(k0, k0 + ATTN_KSUB - 1 > q0) for k0 in range(kg, kg + ATTN_KGROUP * ATTN_KSUB, ATTN_KSUB)
                        if k0 <= q0 + ATTN_QSUB - 1]
                if subs:
                    work.append((hh, q0, subs))
    tiles = [(hh, k0, q0) for hh, q0, subs in work for k0, _ in subs]

    def scores(hh, k0, q0):
        q_sub = qt_ref[q0 // t, hh * LANES:(hh + 1) * LANES, q0 % t:q0 % t + ATTN_QSUB]
        k_sub = k_ref[k0 // t, k0 % t:k0 % t + ATTN_KSUB, hh * LANES:(hh + 1) * LANES]
        return jnp.dot(k_sub, q_sub, preferred_element_type=F32)

    issued = min(ATTN_AHEAD, len(tiles))
    ahead = [scores(*x) for x in tiles[:issued]]
    for hh, q0, subs in work:
        s_subs = []
        for k0, masked in subs:
            s = ahead.pop(0)
            if issued < len(tiles):
                ahead.append(scores(*tiles[issued]))
                issued += 1
            if masked:
                kpos = k0 + lax.broadcasted_iota(jnp.int32, s.shape, 0)
                qpos = q0 + lax.broadcasted_iota(jnp.int32, s.shape, 1)
                s = jnp.where(kpos <= qpos, s, -jnp.inf)
            s_subs.append(s)
        qcols = slice(q0, q0 + ATTN_QSUB)
        m = m_ref[hh, :, qcols]
        m_new = m
        for s in s_subs:
            m_new = jnp.maximum(m_new, jnp.max(s, axis=0, keepdims=True))
        alpha = jnp.exp2(m - m_new)
        p = jnp.concatenate([jnp.exp2((s - m_new).astype(BF16)) for s in s_subs], axis=0)
        m_ref[hh, :, qcols] = m_new
        k_lo = subs[0][0]
        width = len(subs) * ATTN_KSUB
        v_sub = vt_ref[k_lo // t, hh * MLA_V:(hh + 1) * MLA_V, k_lo % t:k_lo % t + width]
        pv = jnp.dot(jnp.concatenate([v_sub, ones[:, :width]], axis=0), p, preferred_element_type=F32)
        acc_ref[hh, :, qcols] = alpha * acc_ref[hh, :, qcols] + pv

    out = jnp.concatenate([acc_ref[hh, :MLA_V] / acc_ref[hh, MLA_V:MLA_V + 1] for hh in range(2)], axis=0)
    o_ref[...] = out.T.astype(BF16)


def _attention(qt, k, vt, b, s):
    tiles, _, t = qt.shape
    pairs = MLA_HEADS // 2
    per_b = s // t
    assert tiles == b * per_b and t % (ATTN_KGROUP * ATTN_KSUB) == 0 and t % ATTN_QSUB == 0
    return pl.pallas_call(
        _attn_kernel,
        grid=(b, pairs),
        in_specs=[pl.BlockSpec((per_b, 2 * LANES, t), lambda bi, j: (bi, j, 0)),
                  pl.BlockSpec((per_b, t, 2 * LANES), lambda bi, j: (bi, 0, j)),
                  pl.BlockSpec((per_b, 2 * MLA_V, t), lambda bi, j: (bi, j, 0))],
        out_specs=pl.BlockSpec((s, 2 * MLA_V), lambda bi, j: (bi, j)),
        out_shape=jax.ShapeDtypeStruct((b * s, MLA_HEADS * MLA_V), BF16),
        scratch_shapes=[pltpu.VMEM((2, 1, s), F32), pltpu.VMEM((2, MLA_V + ATTN_SUM_ROWS, s), F32)],
        compiler_params=_params("parallel", "parallel"),
        name="attn",
    )(qt, k.reshape(tiles, t, k.shape[1]), vt)


def _cumsum_lanes(x):
    lane = _lane_iota(x.shape)
    sh = 1
    while sh < x.shape[-1]:
        x = x + jnp.where(lane >= sh, pltpu.roll(x, sh, 1), 0.0)
        sh *= 2
    return x


def _ssd_kernel(xbc_ref, z_ref, small_ref, conv_w_ref, conv_b_ref, dt_bias_ref, a_log_ref, d_skip_ref, gain_ref,
                y_ref, ext_ref, state_ref):
    tile = xbc_ref.shape[0]
    inner = z_ref.shape[1]
    gstate = SSD_GROUPS * SSD_STATE

    @pl.when(pl.program_id(1) == 0)
    def _():
        ext_ref[0:SSD_CHUNK, :] = jnp.zeros((SSD_CHUNK, ext_ref.shape[1]), BF16)
        state_ref[...] = jnp.zeros(state_ref.shape, F32)

    ext_ref[SSD_CHUNK:SSD_CHUNK + tile, :] = xbc_ref[...]

    n_shift = SSD_CONV - 1
    srow = lax.broadcasted_iota(jnp.int32, (SSD_CHUNK, 2 * SSD_CHUNK), 0)
    scol = lax.broadcasted_iota(jnp.int32, (SSD_CHUNK, 2 * SSD_CHUNK), 1)
    pick = jnp.concatenate([jnp.where(scol == SSD_CHUNK + srow - (d + 1), 1.0, 0.0).astype(BF16)
                            for d in range(n_shift)], axis=0)

    def conv_chunk(c):
        window = ext_ref[c * SSD_CHUNK:(c + 2) * SSD_CHUNK, :]
        shifted = jnp.dot(pick, window, preferred_element_type=F32)
        conv = conv_b_ref[...] + conv_w_ref[n_shift:n_shift + 1, :] * window[SSD_CHUNK:].astype(F32)
        for d in range(n_shift):
            conv = conv + conv_w_ref[n_shift - 1 - d:n_shift - d, :] * shifted[d * SSD_CHUNK:(d + 1) * SSD_CHUNK]
        return conv * jax.nn.sigmoid(conv)

    lane = _lane_iota((tile, LANES))
    head_lane = (lane >= DT_LANE0) & (lane < DT_LANE0 + SSD_HEADS)
    dt = jnp.where(head_lane, jax.nn.softplus(small_ref[...] + dt_bias_ref[...]), 0.0)
    da = dt * (-jnp.exp(a_log_ref[...]))

    sub = lax.broadcasted_iota(jnp.int32, (SSD_CHUNK, SSD_CHUNK), 0)
    lan = lax.broadcasted_iota(jnp.int32, (SSD_CHUNK, SSD_CHUNK), 1)
    causal = lan <= sub
    low_half = lan < SSD_HEAD_DIM
    heads_per_group = SSD_HEADS // SSD_GROUPS

    xc_next = conv_chunk(0)
    for c in range(tile // SSD_CHUNK):
        rows = slice(c * SSD_CHUNK, (c + 1) * SSD_CHUNK)
        cs_row = _cumsum_lanes(da[rows].T[DT_LANE0:DT_LANE0 + SSD_HEADS])
        cs_col = jnp.concatenate([cs_row, jnp.zeros((SSD_CHUNK - SSD_HEADS, SSD_CHUNK), F32)], axis=0).T
        dt_c = dt[rows]
        xc = xc_next
        if c + 1 < tile // SSD_CHUNK:
            xc_next = conv_chunk(c + 1)
        b_t = [xc[:, inner + g * SSD_STATE:inner + (g + 1) * SSD_STATE].T.astype(BF16) for g in range(SSD_GROUPS)]
        c_in = [xc[:, inner + gstate + g * SSD_STATE:inner + gstate + (g + 1) * SSD_STATE].astype(BF16)
                for g in range(SSD_GROUPS)]
        cb = [jnp.dot(c_in[g], b_t[g], preferred_element_type=F32) for g in range(SSD_GROUPS)]
        for pr in range(SSD_HEADS // 2):
            g = (2 * pr) // heads_per_group
            lanes = slice(pr * LANES, (pr + 1) * LANES)
            xs = xc[:, lanes]
            col = [cs_col[:, 2 * pr + i:2 * pr + i + 1] for i in range(2)]
            row = [cs_row[2 * pr + i:2 * pr + i + 1, :] for i in range(2)]
            dtc = [dt_c[:, 2 * pr + i:2 * pr + i + 1] for i in range(2)]
            tot = [cs_col[SSD_CHUNK - 1:SSD_CHUNK, 2 * pr + i:2 * pr + i + 1] for i in range(2)]
            xdt = xs * jnp.where(low_half, dtc[0], dtc[1])
            xdt_b = xdt.astype(BF16)
            y_d = []
            for i in range(2):
                lmat = jnp.exp(jnp.where(causal, col[i] - row[i], -jnp.inf))
                y_d.append(jnp.dot((cb[g] * lmat).astype(BF16), xdt_b, preferred_element_type=F32))
            y_diag = jnp.where(low_half, y_d[0], y_d[1])
            decay = jnp.where(low_half, jnp.exp(tot[0] - col[0]), jnp.exp(tot[1] - col[1]))
            states = jnp.dot(b_t[g], (xdt * decay).astype(BF16), preferred_element_type=F32)
            s_in = state_ref[pr]
            y_off = jnp.dot(c_in[g], s_in.astype(BF16), preferred_element_type=F32)
            y_off = y_off * jnp.where(low_half, jnp.exp(col[0]), jnp.exp(col[1]))
            state_ref[pr] = s_in * jnp.where(low_half, jnp.exp(tot[0]), jnp.exp(tot[1])) + states
            y = y_diag + y_off + xs * d_skip_ref[:, lanes]
            zz = z_ref[rows, lanes].astype(F32)
            y_ref[rows, lanes] = y * (zz * jax.nn.sigmoid(zz))

    ext_ref[0:SSD_CHUNK, :] = ext_ref[tile:tile + SSD_CHUNK, :]

    gw = inner // SSD_GROUPS
    for g in range(SSD_GROUPS):
        lanes = slice(g * gw, (g + 1) * gw)
        y_ref[:, lanes] = _rms(y_ref[:, lanes], gain_ref[:, lanes])


def _ssd(xbc, z, small, p, layer, b, s):
    n, conv_dim = xbc.shape
    inner = z.shape[1]
    tile = min(SSD_TILE, s)
    assert s % tile == 0 and tile % SSD_CHUNK == 0
    per_b = s // tile
    rows = lambda w: pl.BlockSpec((tile, w), lambda bi, si: (bi * per_b + si, 0))
    res = lambda a: _resident(a.shape, layer)
    return pl.pallas_call(
        _ssd_kernel,
        grid=(b, per_b),
        in_specs=[rows(conv_dim), rows(inner), rows(LANES), res(p["conv_w"]), res(p["conv_b"]), res(p["dt_bias"]),
                  res(p["a_log"]), res(p["d_skip"]), res(p["ssd_gain"])],
        out_specs=rows(inner),
        out_shape=jax.ShapeDtypeStruct((n, inner), F32),
        scratch_shapes=[pltpu.VMEM((SSD_CHUNK + tile, conv_dim), BF16),
                        pltpu.VMEM((SSD_HEADS // 2, SSD_STATE, 2 * SSD_HEAD_DIM), F32)],
        compiler_params=_params("parallel", "arbitrary"),
        name="ssd",
    )(xbc, z, small, p["conv_w"], p["conv_b"], p["dt_bias"], p["a_log"], p["d_skip"], p["ssd_gain"])


def _merge_kernel(x_ref, ya_ref, yb_ref, yc_ref, mix_gain_ref, w_gate_ref, w_branch_ref, w_out_ref, o_ref):
    x = x_ref[...]
    d = x.shape[1]
    h = _rms(x, mix_gain_ref[...]).astype(BF16)
    merged = None
    for i, y_ref in enumerate((ya_ref, yb_ref, yc_ref)):
        gate = jax.nn.sigmoid(jnp.dot(h, w_gate_ref[:, i * d:(i + 1) * d], preferred_element_type=F32))
        term = gate * jnp.dot(y_ref[...].astype(BF16), w_branch_ref[i], preferred_element_type=F32)
        merged = term if merged is None else merged + term
    o_ref[...] = x + jnp.dot(merged.astype(BF16), w_out_ref[...], preferred_element_type=F32)


def _merge(x, ya, yb, yc, p, layer):
    n, d = x.shape
    res = lambda a: _resident(a.shape, layer)
    return pl.pallas_call(
        _merge_kernel,
        grid=(n // ROW_TILE,),
        in_specs=[_rows(d), _rows(ya.shape[1]), _rows(yb.shape[1]), _rows(yc.shape[1]),
                  res(p["mix_norm"]), res(p["w_gate"]), res(p["w_branch"]), res(p["w_out"])],
        out_specs=_rows(d),
        out_shape=jax.ShapeDtypeStruct((n, d), F32),
        compiler_params=_params("parallel"),
        name="merge",
    )(x, ya, yb, yc, p["mix_norm"], p["w_gate"], p["w_branch"], p["w_out"])


def _row(a):
    return a[:, None, :]


def _col(a):
    return a[:, :, None]


def _pad_lanes(a, lane0):
    return _row(jnp.pad(a, ((0, 0), (lane0, LANES - lane0 - a.shape[1]))))


def kernel(x, positions, ffn1_norm, ffn1_w_in, ffn1_w_out, mix_norm, w_in, gm_v_norm, gm_w_s, gm_b_s, mla_q_norm, mla_kv_norm, mla_w_uq, mla_w_ukv, mla_q_gain, mla_k_gain, ssd_conv_w, ssd_conv_b, ssd_dt_bias, ssd_a_log, ssd_d, ssd_norm, w_branch, w_out, ffn2_norm, ffn2_w_in, ffn2_w_out):
    b, s, d = x.shape
    depth = w_in.shape[0]
    n = b * s
    gm_w = gm_v_norm.shape[1]
    q_rank = mla_q_norm.shape[1]
    kv_rank = mla_kv_norm.shape[1]
    ssd_inner = ssd_norm.shape[1]
    conv_dim = ssd_conv_b.shape[1]
    widths = (gm_w, q_rank, kv_rank, ssd_inner, conv_dim)

    sizes = (2 * gm_w, q_rank, kv_rank, MLA_ROPE, ssd_inner, conv_dim, SSD_HEADS, N_BRANCH * d)
    offs = [0]
    for w in sizes:
        offs.append(offs[-1] + w)
    w_in_b = w_in.astype(BF16)
    cols = lambda i, j: w_in_b[:, :, offs[i]:offs[j]]
    w_small = jnp.zeros((depth, d, LANES), BF16)
    w_small = w_small.at[:, :, DT_LANE0:DT_LANE0 + SSD_HEADS].set(cols(6, 7))
    w_small = w_small.at[:, :, ROPE_LANE0:ROPE_LANE0 + MLA_ROPE].set(cols(3, 4))

    w_ukv = mla_w_ukv.reshape(depth, kv_rank, MLA_HEADS, MLA_NOPE + MLA_V)
    out_major = lambda w: w.reshape(depth, kv_rank, -1).transpose(0, 2, 1).astype(BF16)
    p = {
        "mix_norm": _row(mix_norm),
        "w_lat": cols(0, 3),
        "w_small": w_small,
        "w_ssd": cols(4, 6),
        "w_gate": cols(7, 8),
        "gm_gain": _row(gm_v_norm),
        "gm_w": gm_w_s,
        "gm_b": jnp.repeat(gm_b_s.transpose(0, 2, 1), gm_w // GM_GROUPS, axis=2),
        "q_norm": _col(mla_q_norm),
        "kv_norm": _col(mla_kv_norm),
        "w_uq": mla_w_uq.transpose(0, 2, 1).astype(BF16),
        "w_uk": out_major(w_ukv[..., :MLA_NOPE]),
        "w_uv": out_major(w_ukv[..., MLA_NOPE:]),
        "q_gain": _col(mla_q_gain * (MLA_QK_DIM ** -0.5 * LOG2_E)),
        "k_gain": _col(mla_k_gain),
        "conv_w": ssd_conv_w,
        "conv_b": _row(ssd_conv_b),
        "dt_bias": _pad_lanes(ssd_dt_bias, DT_LANE0),
        "a_log": _pad_lanes(ssd_a_log, DT_LANE0),
        "d_skip": _row(jnp.repeat(ssd_d, SSD_HEAD_DIM, axis=1)),
        "ssd_gain": _row(ssd_norm),
        "w_branch": w_branch.astype(BF16),
        "w_out": w_out.astype(BF16),
    }
    ffn = [(_row(ffn1_norm), ffn1_w_in.astype(BF16), ffn1_w_out.astype(BF16)),
           (_row(ffn2_norm), ffn2_w_in.astype(BF16), ffn2_w_out.astype(BF16))]

    cosf, sinf = _rope_tables(positions)
    xf = x.reshape(n, d)
    for layer in range(depth):
        xf = _ffn(xf, *ffn[0], layer)
        ya, qt, k, vt, small, z, xbc = _proj(xf, cosf, sinf, p, layer, widths)
        yb = _attention(qt, k, vt, b, s)
        yc = _ssd(xbc, z, small, p, layer, b, s)
        xf = _merge(xf, ya, yb, yc, p, layer)
        xf = _ffn(xf, *ffn[1], layer)
    return xf.reshape(b, s, d)
```
